```python
import jax
import jax.numpy as jnp
from jax import lax
import numpy as np


D_MODEL = 1024
BATCH = 16
SEQ = 4096
DEPTH = 2

D_FF = 2816
A_HEADS = 8
A_KV_HEADS = 2
A_HEAD_DIM = 64
WINDOW = 128
A_BLOCK = 128
B_HEADS = 4
B_KEY_DIM = 128
B_VAL_DIM = 128
B_CHUNK = 64
C_HEADS = 16
C_Q_RANK = 256
C_KV_RANK = 256
C_NOPE = 64
C_ROPE = 32
C_V = 64
C_QBLOCK = 128
ROPE_THETA = 10000.0
LN_EPS = 1e-5
RMS_EPS = 1e-6
DEEPNORM_ALPHA = (2 * DEPTH) ** 0.25
DEEPNORM_BETA = (8 * DEPTH) ** -0.25
N_EVEN = (DEPTH + 1) // 2
N_ODD = DEPTH // 2
N_SUB = 3
A_Q_W = A_HEADS * A_HEAD_DIM
A_KV_W = A_KV_HEADS * A_HEAD_DIM
B_K_W = B_HEADS * B_KEY_DIM
B_V_W = B_HEADS * B_VAL_DIM
HYB_SPLITS = (A_Q_W, A_KV_W, A_KV_W, B_K_W, B_K_W, B_K_W, B_V_W, B_V_W)
HYB_IN = sum(HYB_SPLITS)
HYB_OUT = A_Q_W + B_V_W
MLA_DOWN = C_Q_RANK + C_KV_RANK + C_ROPE
F32 = jnp.float32

kernel_name = 'hybrid_swa_hgrn2_mla_macaron_deepnorm_adaln'


def layer_norm(x, g, b):
    xf = x.astype(F32)
    mu = jnp.mean(xf, -1, keepdims=True)
    var = jnp.mean(jnp.square(xf - mu), -1, keepdims=True)
    return ((xf - mu) * lax.rsqrt(var + LN_EPS) * g + b).astype(x.dtype)


def rms_norm(x, w):
    xf = x.astype(F32)
    return (xf * lax.rsqrt(jnp.mean(xf * xf, -1, keepdims=True) + RMS_EPS) * w).astype(x.dtype)


def swiglu(h, w_gate, w_up, w_down):
    return (jax.nn.silu(h @ w_gate) * (h @ w_up)) @ w_down


def rope(x, pos):
    half = x.shape[-1] // 2
    freqs = ROPE_THETA ** (-jnp.arange(half, dtype=F32) / half)
    ang = pos.astype(F32)[..., None] * freqs
    ang = ang.reshape(ang.shape[:2] + (1,) * (x.ndim - 3) + (half,))
    cos, sin = jnp.cos(ang), jnp.sin(ang)
    x1, x2 = x[..., :half].astype(F32), x[..., half:].astype(F32)
    return jnp.concatenate([x1 * cos - x2 * sin, x1 * sin + x2 * cos], -1).astype(x.dtype)


def alibi_slopes(n):
    return 2.0 ** (-8.0 * jnp.arange(1, n + 1, dtype=F32) / n)


def window_attention(q, k, v, pos, sink):
    bsz, seq = q.shape[:2]
    nb = seq // A_BLOCK
    grp = A_HEADS // A_KV_HEADS

    def band(t):
        tp = jnp.pad(t, [(0, 0), (A_BLOCK, A_BLOCK)] + [(0, 0)] * (t.ndim - 2))
        tb = tp.reshape((bsz, nb + 2, A_BLOCK) + t.shape[2:])
        return jnp.concatenate([tb[:, :-2], tb[:, 1:-1], tb[:, 2:]], axis=2)

    kb, vb, pk = band(k), band(v), band(pos)
    qb = q.reshape(bsz, nb, A_BLOCK, A_KV_HEADS, grp, A_HEAD_DIM)
    pq = pos.reshape(bsz, nb, A_BLOCK)
    s = jnp.einsum('bnqhgd,bnkhd->bnhgqk', qb, kb).astype(F32) * (A_HEAD_DIM ** -0.5)
    qi = jnp.arange(A_BLOCK)[:, None] + A_BLOCK
    ki = jnp.arange(3 * A_BLOCK)[None, :]
    abs_k = jnp.arange(nb)[:, None, None] * A_BLOCK + ki[None] - A_BLOCK
    valid = (jnp.abs(qi - ki) <= WINDOW)[None] & (abs_k >= 0) & (abs_k < seq)
    dist = jnp.abs(pq[..., :, None] - pk[..., None, :]).astype(F32)
    slopes = alibi_slopes(A_HEADS).reshape(A_KV_HEADS, grp)[None, None, :, :, None, None]
    s = s - slopes * dist[:, :, None, None]
    s = jnp.where(valid[None, :, None, None], s, -jnp.inf)
    sink_l = sink.astype(F32).reshape(A_KV_HEADS, grp)[None, None, :, :, None, None]
    m = jnp.maximum(jnp.max(s, -1, keepdims=True), sink_l)
    p = jnp.exp(s - m)
    p = p / (jnp.sum(p, -1, keepdims=True) + jnp.exp(sink_l - m))
    o = jnp.einsum('bnhgqk,bnkhd->bnqhgd', p.astype(v.dtype), vb)
    return o.reshape(bsz, seq, A_Q_W)


def hgrn2_scan(q, k, v, logf):
    bsz, nh, seq, dk = q.shape
    dv = v.shape[-1]
    n = seq // B_CHUNK
    q, k, logf = [t.reshape(bsz, nh, n, B_CHUNK, dk) for t in (q, k, logf)]
    v = v.reshape(bsz, nh, n, B_CHUNK, dv)
    b = jnp.cumsum(logf, axis=3)
    b_mid = b[:, :, :, B_CHUNK // 2 - 1:B_CHUNK // 2]
    b_end = b[:, :, :, -1:]
    a = jnp.einsum('bhnck,bhnsk->bhncs', q * jnp.exp(b - b_mid), k * jnp.exp(b_mid - b))
    tri = jnp.tril(jnp.ones((B_CHUNK, B_CHUNK), bool))
    o_intra = jnp.einsum('bhncs,bhnsv->bhncv', jnp.where(tri, a, 0.0), v)
    d_state = jnp.einsum('bhnck,bhncv->bhnkv', k * jnp.exp(b_end - b), v)
    decay = jnp.exp(b_end[:, :, :, 0])

    def step(s_prev, inp):
        dec, ds = inp
        return dec[..., None] * s_prev + ds, s_prev

    _, s0 = lax.scan(step, jnp.zeros((bsz, nh, dk, dv), q.dtype),
                     (jnp.moveaxis(decay, 2, 0), jnp.moveaxis(d_state, 2, 0)))
    s0 = jnp.moveaxis(s0, 0, 2)
    o_inter = jnp.einsum('bhnck,bhnkv->bhncv', q * jnp.exp(b), s0)
    return (o_intra + o_inter).reshape(bsz, nh, seq, dv)


def hgrn2_bidir(hq, hff, hfb, hi, hg, lb, norm_w):
    bsz, seq = hq.shape[:2]

    def heads(t, d):
        return t.reshape(bsz, seq, B_HEADS, d).transpose(0, 2, 1, 3).astype(F32)

    q = jax.nn.silu(heads(hq, B_KEY_DIM))
    v = heads(hi, B_VAL_DIM)
    lbh = lb.astype(F32).reshape(B_HEADS, 1, B_KEY_DIM)

    def gates(hf):
        f = lbh + (1.0 - lbh) * jax.nn.sigmoid(heads(hf, B_KEY_DIM))
        return 1.0 - f, jnp.log(f)

    k_f, g_f = gates(hff)
    k_b, g_b = gates(hfb)
    flip = lambda t: jnp.flip(t, axis=2)
    o = hgrn2_scan(q, k_f, v, g_f) + flip(hgrn2_scan(flip(q), flip(k_b), flip(v), flip(g_b)))
    o = rms_norm(o, norm_w[:, None, :])
    o = o.transpose(0, 2, 1, 3).reshape(bsz, seq, B_V_W).astype(hg.dtype)
    return o * jax.nn.silu(hg)


def hybrid_mixer(h, pos, w_in, w_out, sink, lb, norm_w):
    bsz, seq, _ = h.shape
    offs = [int(o) for o in np.cumsum(HYB_SPLITS)[:-1]]
    aq, ak, av, bq, bff, bfb, bi, bg = jnp.split(h @ w_in, offs, axis=-1)
    o_a = window_attention(aq.reshape(bsz, seq, A_HEADS, A_HEAD_DIM),
                           ak.reshape(bsz, seq, A_KV_HEADS, A_HEAD_DIM),
                           av.reshape(bsz, seq, A_KV_HEADS, A_HEAD_DIM), pos, sink)
    o_b = hgrn2_bidir(bq, bff, bfb, bi, bg, lb, norm_w)
    return jnp.concatenate([o_a, o_b], axis=-1) @ w_out


def mla(h, pos, w_down, q_norm, kv_norm, w_uq, w_ukv, w_out):
    bsz, seq, _ = h.shape
    cq, ckv, kr = jnp.split(h @ w_down, [C_Q_RANK, C_Q_RANK + C_KV_RANK], axis=-1)
    q = (rms_norm(cq, q_norm) @ w_uq).reshape(bsz, seq, C_HEADS, C_NOPE + C_ROPE)
    q_nope, q_rope = q[..., :C_NOPE], rope(q[..., C_NOPE:], pos)
    kv = (rms_norm(ckv, kv_norm) @ w_ukv).reshape(bsz, seq, C_HEADS, C_NOPE + C_V)
    k_nope, v = kv[..., :C_NOPE], kv[..., C_NOPE:]
    k_rope = rope(kr, pos)
    scale = (C_NOPE + C_ROPE) ** -0.5
    nq = seq // C_QBLOCK
    qn_b = q_nope.reshape(bsz, nq, C_QBLOCK, C_HEADS, C_NOPE).swapaxes(0, 1)
    qr_b = q_rope.reshape(bsz, nq, C_QBLOCK, C_HEADS, C_ROPE).swapaxes(0, 1)

    def block(args):
        qn_i, qr_i = args
        s = (jnp.einsum('bqhd,bkhd->bhqk', qn_i, k_nope)
             + jnp.einsum('bqhr,bkr->bhqk', qr_i, k_rope)).astype(F32) * scale
        p = jax.nn.softmax(s, axis=-1).astype(v.dtype)
        return jnp.einsum('bhqk,bkhd->bqhd', p, v)

    o = lax.map(block, (qn_b, qr_b))
    return o.swapaxes(0, 1).reshape(bsz, seq, C_HEADS * C_V) @ w_out


def modulated_sublayer(x, mod, g, b, fn, res_w):
    shift, scale, gate = mod[:, 0, None, :], mod[:, 1, None, :], mod[:, 2, None, :]
    y = fn(x * (1.0 + scale) + shift)
    return layer_norm(DEEPNORM_ALPHA * x + res_w * (1.0 + gate) * y, g, b)


def setup_inputs(seed: int = 0) -> dict:
    key = jax.random.key(seed)
    ks = jax.random.split(key, 24)
    nrm = lambda k, shape, s: jax.random.normal(k, shape, F32) * s
    d = D_MODEL
    return {
        'x': nrm(ks[0], (BATCH, SEQ, d), 1.0),
        'c': nrm(ks[1], (BATCH, d), 1.0),
        'positions': jnp.broadcast_to(jnp.arange(SEQ, dtype=jnp.int32), (BATCH, SEQ)),
        'ada_w': nrm(ks[2], (DEPTH, d, N_SUB * 3 * d), 0.1 * d ** -0.5),
        'ada_b': nrm(ks[3], (DEPTH, N_SUB * 3 * d), 0.01),
        'ln_g': 1.0 + nrm(ks[4], (DEPTH, N_SUB, d), 0.01),
        'ln_b': nrm(ks[5], (DEPTH, N_SUB, d), 0.01),
        'ffn_w_gate': nrm(ks[6], (DEPTH, 2, d, D_FF), d ** -0.5),
        'ffn_w_up': nrm(ks[7], (DEPTH, 2, d, D_FF), d ** -0.5),
        'ffn_w_down': nrm(ks[8], (DEPTH, 2, D_FF, d), DEEPNORM_BETA * D_FF ** -0.5),
        'hyb_w_in': nrm(ks[9], (N_EVEN, d, HYB_IN), d ** -0.5),
        'hyb_w_out': nrm(ks[10], (N_EVEN, HYB_OUT, d), DEEPNORM_BETA * HYB_OUT ** -0.5),
        'attn_sink': nrm(ks[11], (N_EVEN, A_HEADS), 0.5),
        'hgrn_lb_logits': nrm(ks[12], (DEPTH + 1, B_K_W), 0.1),
        'hgrn_norm_w': 1.0 + nrm(ks[13], (N_EVEN, B_HEADS, B_VAL_DIM), 0.01),
        'mla_w_down': nrm(ks[14], (N_ODD, d, MLA_DOWN), d ** -0.5),
        'mla_q_norm': 1.0 + nrm(ks[15], (N_ODD, C_Q_RANK), 0.01),
        'mla_kv_norm': 1.0 + nrm(ks[16], (N_ODD, C_KV_RANK), 0.01),
        'mla_w_uq': nrm(ks[17], (N_ODD, C_Q_RANK, C_HEADS * (C_NOPE + C_ROPE)), C_Q_RANK ** -0.5),
        'mla_w_ukv': nrm(ks[18], (N_ODD, C_KV_RANK, C_HEADS * (C_NOPE + C_V)), C_KV_RANK ** -0.5),
        'mla_w_out': nrm(ks[19], (N_ODD, C_HEADS * C_V, d), DEEPNORM_BETA * (C_HEADS * C_V) ** -0.5),
    }


def reference(x, c, positions, ada_w, ada_b, ln_g, ln_b, ffn_w_gate, ffn_w_up, ffn_w_down,
              hyb_w_in, hyb_w_out, attn_sink, hgrn_lb_logits, hgrn_norm_w,
              mla_w_down, mla_q_norm, mla_kv_norm, mla_w_uq, mla_w_ukv, mla_w_out):
    bsz = x.shape[0]
    cond = jax.nn.silu(c)
    lb_all = jnp.cumsum(jax.nn.softmax(hgrn_lb_logits.astype(F32), axis=0), axis=0)
    for layer in range(DEPTH):
        mod = (cond @ ada_w[layer] + ada_b[layer]).reshape(bsz, N_SUB, 3, D_MODEL)
        ffn_pre = lambda h, l=layer: swiglu(h, ffn_w_gate[l, 0], ffn_w_up[l, 0], ffn_w_down[l, 0])
        ffn_post = lambda h, l=layer: swiglu(h, ffn_w_gate[l, 1], ffn_w_up[l, 1], ffn_w_down[l, 1])
        if layer % 2 == 0:
            e = layer // 2
            mixer = lambda h, e=e, l=layer: hybrid_mixer(h, positions, hyb_w_in[e], hyb_w_out[e],
                                                         attn_sink[e], lb_all[l], hgrn_norm_w[e])
        else:
            o = layer // 2
            mixer = lambda h, o=o: mla(h, positions, mla_w_down[o], mla_q_norm[o], mla_kv_norm[o],
                                       mla_w_uq[o], mla_w_ukv[o], mla_w_out[o])
        x = modulated_sublayer(x, mod[:, 0], ln_g[layer, 0], ln_b[layer, 0], ffn_pre, 0.5)
        x = modulated_sublayer(x, mod[:, 1], ln_g[layer, 1], ln_b[layer, 1], mixer, 1.0)
        x = modulated_sublayer(x, mod[:, 2], ln_g[layer, 2], ln_b[layer, 2], ffn_post, 0.5)
    return x
```

```python
import functools

import jax
import jax.numpy as jnp
from jax import lax
from jax.experimental import pallas as pl
from jax.experimental.pallas import tpu as pltpu

F32 = jnp.float32
BF16 = jnp.bfloat16

D_MODEL = 1024
DEPTH = 2
D_FF = 2816
A_HEADS = 8
A_KV_HEADS = 2
A_HEAD_DIM = 64
A_GROUP = A_HEADS // A_KV_HEADS
WINDOW = 128
A_BLOCK = 128
B_HEADS = 4
B_KEY_DIM = 128
B_VAL_DIM = 128
B_CHUNK = 64
C_HEADS = 16
C_Q_RANK = 256
C_KV_RANK = 256
C_NOPE = 64
C_ROPE = 32
C_V = 64
ROPE_THETA = 10000.0
LN_EPS = 1e-5
RMS_EPS = 1e-6
ALPHA = (2 * DEPTH) ** 0.25
N_SUB = 3
A_Q_W = A_HEADS * A_HEAD_DIM
A_KV_W = A_KV_HEADS * A_HEAD_DIM
B_W = B_HEADS * B_KEY_DIM
LANES = 128
VMEM_LIMIT = 56 * 1024 * 1024

_NT = (((1,), (1,)), ((), ()))
_TN = (((0,), (0,)), ((), ()))


def _params(*sem):
    return pltpu.CompilerParams(dimension_semantics=sem, vmem_limit_bytes=VMEM_LIMIT)


def _resident(shape):
    zeros = (0,) * len(shape)
    return pl.BlockSpec(shape, lambda *_: zeros, pipeline_mode=pl.Buffered(1))


def _dot(a, b):
    return jnp.dot(a, b, preferred_element_type=F32)


def _modulate(x, mod_ref):
    return x * (1.0 + mod_ref[0, 1:2, :]) + mod_ref[0, 0:1, :]


def _residual_ln(x, y, mod_ref, g_ref, b_ref, res_w):
    r = ALPHA * x + res_w * (1.0 + mod_ref[0, 2:3, :]) * y
    mu = jnp.mean(r, -1, keepdims=True)
    d = r - mu
    var = jnp.mean(d * d, -1, keepdims=True)
    return d * lax.rsqrt(var + LN_EPS) * g_ref[...] + b_ref[...]


def _silu(v):
    return v * jax.nn.sigmoid(v)


def _mod_body(c_ref, w_ref, b_ref, o_ref):
    cond = _silu(c_ref[...])
    o_ref[0] = jnp.dot(cond, w_ref[0], preferred_element_type=F32,
                       precision=lax.Precision.HIGHEST) + b_ref[0]


def _modulation(c, ada_w, ada_b):
    bsz, d = c.shape
    n = ada_w.shape[-1]
    tn = 1152
    return pl.pallas_call(
        _mod_body,
        grid=(DEPTH, n // tn),
        in_specs=[pl.BlockSpec((bsz, d), lambda l, j: (0, 0)),
                  pl.BlockSpec((1, d, tn), lambda l, j: (l, 0, j)),
                  pl.BlockSpec((1, 1, tn), lambda l, j: (l, 0, j))],
        out_specs=pl.BlockSpec((1, bsz, tn), lambda l, j: (l, 0, j)),
        out_shape=jax.ShapeDtypeStruct((DEPTH, bsz, n), F32),
        compiler_params=_params("parallel", "parallel"),
        name="adaln_mod",
    )(c, ada_w, ada_b.reshape(DEPTH, 1, n))


def _ffn_body(x_ref, mod_ref, wg_ref, wu_ref, wd_ref, g_ref, b_ref, o_ref):
    x = x_ref[0]
    h = _modulate(x, mod_ref).astype(BF16)
    gate = _dot(h, wg_ref[...])
    up = _dot(h, wu_ref[...])
    act = (_silu(gate) * up).astype(BF16)
    y = _dot(act, wd_ref[...])
    o_ref[0] = _residual_ln(x, y, mod_ref, g_ref, b_ref, 0.5)


def _ffn_sublayer(x, mod, wg, wu, wd, g, b, tm):
    bsz, seq, d = x.shape
    tok = pl.BlockSpec((1, tm, d), lambda i, j: (i, j, 0))
    return pl.pallas_call(
        _ffn_body,
        grid=(bsz, seq // tm),
        in_specs=[tok, pl.BlockSpec((1, 3, d), lambda i, j: (i, 0, 0)),
                  _resident(wg.shape), _resident(wu.shape), _resident(wd.shape),
                  _resident(g.shape), _resident(b.shape)],
        out_specs=tok,
        out_shape=jax.ShapeDtypeStruct(x.shape, F32),
        compiler_params=_params("parallel", "parallel"),
        name="ffn_sublayer",
    )(x, mod, wg, wu, wd, g, b)


def _hyb_in_body(layer, x_ref, mod_ref, w_ref, lbl_ref,
                 aq_ref, ak_ref, av_ref, hq_ref, kf_ref, gf_ref, kb_ref, gb_ref, hv_ref, hg_ref):
    h = _modulate(x_ref[0], mod_ref).astype(BF16)
    p = _dot(h, w_ref[...])
    col = lambda j: p[:, j * B_W:(j + 1) * B_W]
    aq_ref[0] = (col(0) * (A_HEAD_DIM ** -0.5)).astype(BF16)
    ak_ref[0] = col(1).astype(BF16)
    av_ref[0] = col(2).astype(BF16)
    hq_ref[0] = _silu(col(3))
    logits = lbl_ref[...]
    e = jnp.exp(logits - jnp.max(logits, 0, keepdims=True))
    lb = jnp.sum(e[0:layer + 1], 0, keepdims=True) / jnp.sum(e, 0, keepdims=True)
    for src, k_ref, g_ref in ((4, kf_ref, gf_ref), (5, kb_ref, gb_ref)):
        f = lb + (1.0 - lb) * jax.nn.sigmoid(col(src))
        k_ref[0] = 1.0 - f
        g_ref[0] = jnp.log(f)
    hv_ref[0] = col(6)
    hg_ref[0] = _silu(col(7))


def _hybrid_in(x, mod, w, lb_logits, layer, tm):
    bsz, seq, d = x.shape
    tok = pl.BlockSpec((1, tm, d), lambda i, j: (i, j, 0))
    slab = pl.BlockSpec((1, tm, B_W), lambda i, j: (i, j, 0))
    sds = lambda dt: jax.ShapeDtypeStruct((bsz, seq, B_W), dt)
    return pl.pallas_call(
        functools.partial(_hyb_in_body, layer),
        grid=(bsz, seq // tm),
        in_specs=[tok, pl.BlockSpec((1, 3, d), lambda i, j: (i, 0, 0)),
                  _resident(w.shape), _resident(lb_logits.shape)],
        out_specs=[slab] * 10,
        out_shape=[sds(BF16)] * 3 + [sds(F32)] * 7,
        compiler_params=_params("parallel", "parallel"),
        name="hybrid_in",
    )(x, mod, w, lb_logits)


def _win_body(seq, sink_ref, q_ref, km_ref, k0_ref, kp_ref, vm_ref, v0_ref, vp_ref,
              pc_ref, pm_ref, p0_ref, pp_ref, o_ref):
    i = pl.program_id(1)
    blk = A_BLOCK
    gw = A_GROUP * A_HEAD_DIM
    q = q_ref[0]
    k3 = jnp.concatenate([km_ref[0], k0_ref[0], kp_ref[0]], axis=0)
    v3 = jnp.concatenate([vm_ref[0], v0_ref[0], vp_ref[0]], axis=0)
    pk = jnp.concatenate([pm_ref[0], p0_ref[0], pp_ref[0]], axis=1)
    dist = jnp.abs(pc_ref[0] - pk).astype(F32)
    qi = lax.broadcasted_iota(jnp.int32, (blk, 3 * blk), 0) + blk
    ki = lax.broadcasted_iota(jnp.int32, (blk, 3 * blk), 1)
    abs_k = i * blk + ki - blk
    valid = (jnp.abs(qi - ki) <= WINDOW) & (abs_k >= 0) & (abs_k < seq)
    head_of_lane = lax.broadcasted_iota(jnp.int32, (blk, gw), 1) // A_HEAD_DIM
    for j in range(A_KV_HEADS):
        qg = q[:, j * gw:(j + 1) * gw]
        kt = k3[:, j * gw:(j + 1) * gw]
        vt = v3[:, j * gw:(j + 1) * gw]
        qs = jnp.concatenate([jnp.where(head_of_lane == g, qg, jnp.zeros_like(qg))
                              for g in range(A_GROUP)], axis=0)
        s = lax.dot_general(qs, kt, _NT, preferred_element_type=F32)
        probs = []
        for g in range(A_GROUP):
            hd = j * A_GROUP + g
            slope = 2.0 ** (-8.0 * (hd + 1) / A_HEADS)
            sg = s[g * blk:(g + 1) * blk] - slope * dist
            sg = jnp.where(valid, sg, -jnp.inf)
            sk = sink_ref[hd]
            m = jnp.maximum(jnp.max(sg, -1, keepdims=True), sk)
            p = jnp.exp(sg - m)
            p = p / (jnp.sum(p, -1, keepdims=True) + jnp.exp(sk - m))
            probs.append(p.astype(BF16))
        o = _dot(jnp.concatenate(probs, axis=0), vt)
        acc = jnp.zeros((blk, gw), F32)
        for g in range(A_GROUP):
            acc = acc + jnp.where(head_of_lane == g, o[g * blk:(g + 1) * blk], 0.0)
        o_ref[0, :, j * gw:(j + 1) * gw] = acc.astype(BF16)


def _window_attention(aq, ak, av, pos, sink):
    bsz, seq, w = aq.shape
    nb = seq // A_BLOCK
    cur = lambda i, j: (i, j, 0)
    prv = lambda i, j: (i, jnp.maximum(j - 1, 0), 0)
    nxt = lambda i, j: (i, jnp.minimum(j + 1, nb - 1), 0)
    tok = lambda f: pl.BlockSpec((1, A_BLOCK, w), f)
    row = lambda f: pl.BlockSpec((1, 1, A_BLOCK), lambda i, j: (f(i, j)[0], 0, f(i, j)[1]))
    return pl.pallas_call(
        functools.partial(_win_body, seq),
        grid=(bsz, nb),
        in_specs=[pl.BlockSpec(memory_space=pltpu.SMEM),
                  tok(cur), tok(prv), tok(cur), tok(nxt), tok(prv), tok(cur), tok(nxt),
                  pl.BlockSpec((1, A_BLOCK, 1), cur), row(prv), row(cur), row(nxt)],
        out_specs=tok(cur),
        out_shape=jax.ShapeDtypeStruct((bsz, seq, w), BF16),
        compiler_params=_params("parallel", "parallel"),
        name="window_attention",
    )(sink, aq, ak, ak, ak, av, av, av,
      pos.reshape(bsz, seq, 1), pos.reshape(bsz, 1, seq), pos.reshape(bsz, 1, seq), pos.reshape(bsz, 1, seq))


def _cumsum_rows(tri, logf):
    hi = logf.astype(BF16)
    lo = (logf - hi.astype(F32)).astype(BF16)
    return _dot(tri, hi) + _dot(tri, lo)


def _hgrn_chunk(rows, tri_cum, tri_keep, mid, end, q_ref, k_ref, g_ref, v_ref, st_ref, o_ref):
    b_all = _cumsum_rows(tri_cum, g_ref[0, rows, :])
    for h in range(B_HEADS):
        lanes = slice(h * B_KEY_DIM, (h + 1) * B_KEY_DIM)
        b = b_all[:, lanes]
        b_mid = b[mid:mid + 1]
        b_end = b[end:end + 1]
        q = q_ref[0, rows, lanes]
        k = k_ref[0, rows, lanes]
        v = v_ref[0, rows, lanes].astype(BF16)
        a = lax.dot_general((q * jnp.exp(b - b_mid)).astype(BF16), (k * jnp.exp(b_mid - b)).astype(BF16),
                            _NT, preferred_element_type=F32)
        o = _dot(jnp.where(tri_keep, a, 0.0).astype(BF16), v)
        st = st_ref[h]
        o = o + lax.dot_general((q * jnp.exp(b)).astype(BF16), st.astype(BF16), _NT,
                                preferred_element_type=F32)
        o_ref[0, rows, lanes] = o
        d_st = lax.dot_general(v, (k * jnp.exp(b_end - b)).astype(BF16), _TN, preferred_element_type=F32)
        st_ref[h] = st * jnp.exp(b_end) + d_st


def _hgrn_body(n_chunks, qf_ref, kf_ref, gf_ref, vf_ref, qb_ref, kb_ref, gb_ref, vb_ref,
               of_ref, ob_ref, sf_ref, sb_ref):
    @pl.when(pl.program_id(1) == 0)
    def _():
        sf_ref[...] = jnp.zeros_like(sf_ref)
        sb_ref[...] = jnp.zeros_like(sb_ref)

    r = lax.broadcasted_iota(jnp.int32, (B_CHUNK, B_CHUNK), 0)
    c = lax.broadcasted_iota(jnp.int32, (B_CHUNK, B_CHUNK), 1)
    lower = r >= c
    upper = r <= c
    lower_ones = lower.astype(BF16)
    upper_ones = upper.astype(BF16)

    def step(t, carry):
        rows_f = pl.ds(pl.multiple_of(t * B_CHUNK, B_CHUNK), B_CHUNK)
        _hgrn_chunk(rows_f, lower_ones, lower, B_CHUNK // 2 - 1, B_CHUNK - 1,
                    qf_ref, kf_ref, gf_ref, vf_ref, sf_ref, of_ref)
        rows_b = pl.ds(pl.multiple_of((n_chunks - 1 - t) * B_CHUNK, B_CHUNK), B_CHUNK)
        _hgrn_chunk(rows_b, upper_ones, upper, B_CHUNK // 2, 0,
                    qb_ref, kb_ref, gb_ref, vb_ref, sb_ref, ob_ref)
        return carry

    lax.fori_loop(0, n_chunks, step, 0)


def _hgrn2(hq, kf, gf, kb, gb, hv, tb):
    bsz, seq, w = hq.shape
    n = seq // tb
    fwd = pl.BlockSpec((1, tb, w), lambda i, j: (i, j, 0))
    bwd = pl.BlockSpec((1, tb, w), lambda i, j: (i, n - 1 - j, 0))
    out = jax.ShapeDtypeStruct((bsz, seq, w), F32)
    state = pltpu.VMEM((B_HEADS, B_VAL_DIM, B_KEY_DIM), F32)
    return pl.pallas_call(
        functools.partial(_hgrn_body, tb // B_CHUNK),
        grid=(bsz, n),
        in_specs=[fwd] * 4 + [bwd] * 4,
        out_specs=[fwd, bwd],
        out_shape=[out, out],
        scratch_shapes=[state, state],
        compiler_params=_params("parallel", "arbitrary"),
        name="hgrn2_scan",
    )(hq, kf, gf, hv, hq, kb, gb, hv)


def _hyb_out_body(x_ref, mod_ref, oa_ref, of_ref, ob_ref, hg_ref, nw_ref, w_ref, g_ref, b_ref, o_ref):
    o = of_ref[0] + ob_ref[0]
    heads = []
    for h in range(B_HEADS):
        oh = o[:, h * B_VAL_DIM:(h + 1) * B_VAL_DIM]
        heads.append(oh * lax.rsqrt(jnp.mean(oh * oh, -1, keepdims=True) + RMS_EPS))
    ohg = (jnp.concatenate(heads, axis=1) * nw_ref[...] * hg_ref[0]).astype(BF16)
    y = _dot(oa_ref[0], w_ref[0:A_Q_W, :]) + _dot(ohg, w_ref[A_Q_W:, :])
    o_ref[0] = _residual_ln(x_ref[0], y, mod_ref, g_ref, b_ref, 1.0)


def _hybrid_out(x, mod, oa, of, ob, hg, nw, w, g, b, tm):
    bsz, seq, d = x.shape
    tok = pl.BlockSpec((1, tm, d), lambda i, j: (i, j, 0))
    slab = pl.BlockSpec((1, tm, B_W), lambda i, j: (i, j, 0))
    return pl.pallas_call(
        _hyb_out_body,
        grid=(bsz, seq // tm),
        in_specs=[tok, pl.BlockSpec((1, 3, d), lambda i, j: (i, 0, 0)), slab, slab, slab, slab,
                  _resident(nw.shape), _resident(w.shape), _resident(g.shape), _resident(b.shape)],
        out_specs=tok,
        out_shape=jax.ShapeDtypeStruct(x.shape, F32),
        compiler_params=_params("parallel", "parallel"),
        name="hybrid_out",
    )(x, mod, oa, of, ob, hg, nw, w, g, b)


def _rms(x, w_ref):
    return (x * lax.rsqrt(jnp.mean(x * x, -1, keepdims=True) + RMS_EPS) * w_ref[...]).astype(BF16)


def _mla_in_body(x_ref, mod_ref, pos_ref, freq_ref, sgn_ref, wd_ref, qn_ref, kvn_ref, wq_ref, wkv_ref,
                 q_ref, k_ref, v_ref):
    h = _modulate(x_ref[0], mod_ref).astype(BF16)
    down = _dot(h, wd_ref[...])
    ang = pos_ref[0].astype(F32) * freq_ref[...]
    cos = jnp.cos(ang)
    sin = jnp.sin(ang) * sgn_ref[...]
    off = C_Q_RANK + C_KV_RANK
    k_rope = down[:, off:off + LANES] * cos + down[:, off + LANES:off + 2 * LANES] * sin
    cq = _rms(down[:, :C_Q_RANK], qn_ref)
    ckv = _rms(down[:, C_Q_RANK:off], kvn_ref)
    qq = _dot(cq, wq_ref[...])
    kv = _dot(ckv, wkv_ref[...])
    scale = (C_NOPE + C_ROPE) ** -0.5
    nh = C_HEADS * LANES
    for hd in range(C_HEADS):
        lanes = slice(hd * LANES, (hd + 1) * LANES)
        q_ref[0, hd] = ((qq[:, lanes] * cos + qq[:, nh + hd * LANES:nh + (hd + 1) * LANES] * sin) * scale).astype(BF16)
        k_ref[0, hd] = (kv[:, lanes] + k_rope).astype(BF16)
    for pr in range(C_HEADS // 2):
        v_ref[0, pr] = kv[:, nh + pr * LANES:nh + (pr + 1) * LANES].astype(BF16)


def _mla_in(x, mod, pos, freq, sgn, wd, qn, kvn, wq, wkv, tm):
    bsz, seq, d = x.shape
    tok = pl.BlockSpec((1, tm, d), lambda i, j: (i, j, 0))
    heads = lambda n: pl.BlockSpec((1, n, tm, LANES), lambda i, j: (i, 0, j, 0))
    sds = lambda n: jax.ShapeDtypeStruct((bsz, n, seq, LANES), BF16)
    return pl.pallas_call(
        _mla_in_body,
        grid=(bsz, seq // tm),
        in_specs=[tok, pl.BlockSpec((1, 3, d), lambda i, j: (i, 0, 0)),
                  pl.BlockSpec((1, tm, 1), lambda i, j: (i, j, 0)),
                  _resident(freq.shape), _resident(sgn.shape), _resident(wd.shape),
                  _resident(qn.shape), _resident(kvn.shape), _resident(wq.shape), _resident(wkv.shape)],
        out_specs=[heads(C_HEADS), heads(C_HEADS), heads(C_HEADS // 2)],
        out_shape=[sds(C_HEADS), sds(C_HEADS), sds(C_HEADS // 2)],
        compiler_params=_params("parallel", "parallel"),
        name="mla_in",
    )(x, mod, pos.reshape(bsz, seq, 1), freq, sgn, wd, qn, kvn, wq, wkv)


def _mla_attn_body(q_ref, k_ref, v_ref, o_ref):
    v = v_ref[0, 0]
    outs = []
    for e in range(2):
        s = lax.dot_general(q_ref[0, e], k_ref[0, e], _NT, preferred_element_type=F32)
        p = jnp.exp(s - jnp.max(s, -1, keepdims=True))
        den = jnp.sum(p, -1, keepdims=True)
        outs.append(_dot(p.astype(BF16), v) / den)
    lane = lax.broadcasted_iota(jnp.int32, outs[0].shape, 1)
    o_ref[0] = jnp.where(lane < C_V, outs[0], outs[1]).astype(BF16)


def _mla_attention(q, k, v, tq):
    bsz, nh, seq, _ = q.shape
    return pl.pallas_call(
        _mla_attn_body,
        grid=(bsz, nh // 2, seq // tq),
        in_specs=[pl.BlockSpec((1, 2, tq, LANES), lambda b, p, i: (b, p, i, 0)),
                  pl.BlockSpec((1, 2, seq, LANES), lambda b, p, i: (b, p, 0, 0)),
                  pl.BlockSpec((1, 1, seq, LANES), lambda b, p, i: (b, p, 0, 0))],
        out_specs=pl.BlockSpec((1, tq, LANES), lambda b, p, i: (b, i, p)),
        out_shape=jax.ShapeDtypeStruct((bsz, seq, nh * C_V), BF16),
        compiler_params=_params("parallel", "parallel", "parallel"),
        name="mla_attention",
    )(q, k, v)


def _proj_out_body(x_ref, mod_ref, o_ref_in, w_ref, g_ref, b_ref, o_ref):
    y = _dot(o_ref_in[0], w_ref[...])
    o_ref[0] = _residual_ln(x_ref[0], y, mod_ref, g_ref, b_ref, 1.0)


def _proj_out(x, mod, o, w, g, b, tm):
    bsz, seq, d = x.shape
    tok = pl.BlockSpec((1, tm, d), lambda i, j: (i, j, 0))
    return pl.pallas_call(
        _proj_out_body,
        grid=(bsz, seq // tm),
        in_specs=[tok, pl.BlockSpec((1, 3, d), lambda i, j: (i, 0, 0)),
                  pl.BlockSpec((1, tm, o.shape[-1]), lambda i, j: (i, j, 0)),
                  _resident(w.shape), _resident(g.shape), _resident(b.shape)],
        out_specs=tok,
        out_shape=jax.ShapeDtypeStruct(x.shape, F32),
        compiler_params=_params("parallel", "parallel"),
        name="mla_out",
    )(x, mod, o, w, g, b)


def _hybrid_w_in(w):
    d = w.shape[0]
    aq = w[:, :A_Q_W]
    ak = w[:, A_Q_W:A_Q_W + A_KV_W].reshape(d, A_KV_HEADS, 1, A_HEAD_DIM)
    av = w[:, A_Q_W + A_KV_W:A_Q_W + 2 * A_KV_W].reshape(d, A_KV_HEADS, 1, A_HEAD_DIM)
    tile = lambda t: jnp.broadcast_to(t, (d, A_KV_HEADS, A_GROUP, A_HEAD_DIM)).reshape(d, A_Q_W)
    return jnp.concatenate([aq, tile(ak), tile(av), w[:, A_Q_W + 2 * A_KV_W:]], axis=1).astype(BF16)


def _swap_halves(t):
    half = t.shape[-1] // 2
    return jnp.concatenate([t[..., half:], t[..., :half]], axis=-1)


def _mla_weights(w_down, w_uq, w_ukv):
    d = w_down.shape[0]
    off = C_Q_RANK + C_KV_RANK
    pad_rope = lambda t: jnp.pad(t, [(0, 0)] * (t.ndim - 1) + [(C_NOPE, LANES - C_NOPE - C_ROPE)])
    kr = w_down[:, off:]
    wd = jnp.concatenate([w_down[:, :off], pad_rope(kr), pad_rope(_swap_halves(kr))], axis=1)
    uq = w_uq.reshape(C_Q_RANK, C_HEADS, C_NOPE + C_ROPE)
    q_main = jnp.pad(uq, [(0, 0), (0, 0), (0, LANES - C_NOPE - C_ROPE)])
    q_swap = pad_rope(_swap_halves(uq[..., C_NOPE:]))
    wq = jnp.concatenate([q_main.reshape(C_Q_RANK, -1), q_swap.reshape(C_Q_RANK, -1)], axis=1)
    ukv = w_ukv.reshape(C_KV_RANK, C_HEADS, C_NOPE + C_V)
    k_main = jnp.pad(ukv[..., :C_NOPE], [(0, 0), (0, 0), (0, LANES - C_NOPE)])
    wkv = jnp.concatenate([k_main.reshape(C_KV_RANK, -1), ukv[..., C_NOPE:].reshape(C_KV_RANK, -1)], axis=1)
    return wd.astype(BF16), wq.astype(BF16), wkv.astype(BF16)


def _rope_tables():
    half = C_ROPE // 2
    freqs = ROPE_THETA ** (-jnp.arange(half, dtype=F32) / half)
    zeros = jnp.zeros((C_NOPE,), F32)
    tail = jnp.zeros((LANES - C_NOPE - C_ROPE,), F32)
    freq = jnp.concatenate([zeros, freqs, freqs, tail]).reshape(1, LANES)
    ones = jnp.ones((half,), F32)
    sgn = jnp.concatenate([zeros, -ones, ones, tail]).reshape(1, LANES)
    return freq, sgn


def kernel(x, c, positions, ada_w, ada_b, ln_g, ln_b, ffn_w_gate, ffn_w_up, ffn_w_down,
           hyb_w_in, hyb_w_out, attn_sink, hgrn_lb_logits, hgrn_norm_w,
           mla_w_down, mla_q_norm, mla_kv_norm, mla_w_uq, mla_w_ukv, mla_w_out):
    bsz, seq, d = x.shape
    tm = min(512, seq)
    mod_all = _modulation(c, ada_w, ada_b).reshape(DEPTH, bsz, N_SUB, 3, d)
    row = lambda t: t.reshape(1, -1)
    freq, sgn = _rope_tables()
    for layer in range(DEPTH):
        mod = lambda j, l=layer: mod_all[l, :, j]
        ffn = lambda xx, j, s, l=layer: _ffn_sublayer(
            xx, mod(s), ffn_w_gate[l, j].astype(BF16), ffn_w_up[l, j].astype(BF16),
            ffn_w_down[l, j].astype(BF16), row(ln_g[l, s]), row(ln_b[l, s]), tm)
        x = ffn(x, 0, 0)
        if layer % 2 == 0:
            e = layer // 2
            aq, ak, av, hq, kf, gf, kb, gb, hv, hg = _hybrid_in(
                x, mod(1), _hybrid_w_in(hyb_w_in[e]), hgrn_lb_logits.astype(F32), layer, tm)
            oa = _window_attention(aq, ak, av, positions, attn_sink[e].astype(F32))
            of, ob = _hgrn2(hq, kf, gf, kb, gb, hv, tm)
            x = _hybrid_out(x, mod(1), oa, of, ob, hg, row(hgrn_norm_w[e]), hyb_w_out[e].astype(BF16),
                            row(ln_g[layer, 1]), row(ln_b[layer, 1]), tm)
        else:
            o = layer // 2
            wd, wq, wkv = _mla_weights(mla_w_down[o], mla_w_uq[o], mla_w_ukv[o])
            q, k, v = _mla_in(x, mod(1), positions, freq, sgn, wd, row(mla_q_norm[o]), row(mla_kv_norm[o]),
                              wq, wkv, tm)
            att = _mla_attention(q, k, v, tm)
            x = _proj_out(x, mod(1), att, mla_w_out[o].astype(BF16), row(ln_g[layer, 1]), row(ln_b[layer, 1]), tm)
        x = ffn(x, 1, 2)
    return x
```

```python
import functools

import jax
import jax.numpy as jnp
from jax import lax
from jax.experimental import pallas as pl
from jax.experimental.pallas import tpu as pltpu

F32 = jnp.float32
BF16 = jnp.bfloat16

D_MODEL = 1024
DEPTH = 2
D_FF = 2816
A_HEADS = 8
A_KV_HEADS = 2
A_HEAD_DIM = 64
A_GROUP = A_HEADS // A_KV_HEADS
WINDOW = 128
A_BLOCK = 128
B_HEADS = 4
B_KEY_DIM = 128
B_VAL_DIM = 128
B_CHUNK = 64
C_HEADS = 16
C_Q_RANK = 256
C_KV_RANK = 256
C_NOPE = 64
C_ROPE = 32
C_V = 64
ROPE_THETA = 10000.0
LN_EPS = 1e-5
RMS_EPS = 1e-6
LOG2_E = 1.4426950408889634
ALPHA = (2 * DEPTH) ** 0.25
N_SUB = 3
A_Q_W = A_HEADS * A_HEAD_DIM
A_KV_W = A_KV_HEADS * A_HEAD_DIM
B_W = B_HEADS * B_KEY_DIM
LANES = 128
VMEM_LIMIT = 56 * 1024 * 1024

_NT = (((1,), (1,)), ((), ()))
_TN = (((0,), (0,)), ((), ()))


def _params(*sem):
    return pltpu.CompilerParams(dimension_semantics=sem, vmem_limit_bytes=VMEM_LIMIT)


def _resident(shape):
    zeros = (0,) * len(shape)
    return pl.BlockSpec(shape, lambda *_: zeros, pipeline_mode=pl.Buffered(1))


def _dot(a, b):
    return jnp.dot(a, b, preferred_element_type=F32)


def _modulate(x, mod_ref):
    return x * (1.0 + mod_ref[0, 1:2, :]) + mod_ref[0, 0:1, :]


def _residual_ln(x, y, mod_ref, g_ref, b_ref, res_w):
    r = ALPHA * x + res_w * (1.0 + mod_ref[0, 2:3, :]) * y
    mu = jnp.mean(r, -1, keepdims=True)
    d = r - mu
    var = jnp.mean(d * d, -1, keepdims=True)
    return d * lax.rsqrt(var + LN_EPS) * g_ref[...] + b_ref[...]


def _silu(v):
    return v * jax.nn.sigmoid(v)


def _mod_body(c_ref, w_ref, b_ref, o_ref):
    cond = _silu(c_ref[...])
    o_ref[0] = jnp.dot(cond, w_ref[0], preferred_element_type=F32,
                       precision=lax.Precision.HIGHEST) + b_ref[0]


def _modulation(c, ada_w, ada_b):
    bsz, d = c.shape
    n = ada_w.shape[-1]
    tn = 1152
    return pl.pallas_call(
        _mod_body,
        grid=(DEPTH, n // tn),
        in_specs=[pl.BlockSpec((bsz, d), lambda l, j: (0, 0)),
                  pl.BlockSpec((1, d, tn), lambda l, j: (l, 0, j)),
                  pl.BlockSpec((1, 1, tn), lambda l, j: (l, 0, j))],
        out_specs=pl.BlockSpec((1, bsz, tn), lambda l, j: (l, 0, j)),
        out_shape=jax.ShapeDtypeStruct((DEPTH, bsz, n), F32),
        compiler_params=_params("parallel", "parallel"),
        name="adaln_mod",
    )(c, ada_w, ada_b.reshape(DEPTH, 1, n))


def _ffn_body(x_ref, mod_ref, wg_ref, wu_ref, wd_ref, g_ref, b_ref, o_ref):
    x = x_ref[0]
    h = _modulate(x, mod_ref).astype(BF16)
    gate = _dot(h, wg_ref[...])
    up = _dot(h, wu_ref[...])
    act = (_silu(gate) * up).astype(BF16)
    y = _dot(act, wd_ref[...])
    o_ref[0] = _residual_ln(x, y, mod_ref, g_ref, b_ref, 0.5)


def _ffn_sublayer(x, mod, wg, wu, wd, g, b, tm):
    bsz, seq, d = x.shape
    tok = pl.BlockSpec((1, tm, d), lambda i, j: (i, j, 0))
    return pl.pallas_call(
        _ffn_body,
        grid=(bsz, seq // tm),
        in_specs=[tok, pl.BlockSpec((1, 3, d), lambda i, j: (i, 0, 0)),
                  _resident(wg.shape), _resident(wu.shape), _resident(wd.shape),
                  _resident(g.shape), _resident(b.shape)],
        out_specs=tok,
        out_shape=jax.ShapeDtypeStruct(x.shape, F32),
        compiler_params=_params("parallel", "parallel"),
        name="ffn_sublayer",
    )(x, mod, wg, wu, wd, g, b)


def _hyb_in_body(layer, x_ref, mod_ref, w_ref, lbl_ref,
                 aq_ref, ak_ref, av_ref, hq_ref, kf_ref, gf_ref, kb_ref, gb_ref, hv_ref, hg_ref):
    h = _modulate(x_ref[0], mod_ref).astype(BF16)
    p = _dot(h, w_ref[...])
    col = lambda j: p[:, j * B_W:(j + 1) * B_W]
    aq_ref[0] = (col(0) * (A_HEAD_DIM ** -0.5)).astype(BF16)
    ak_ref[0] = col(1).astype(BF16)
    av_ref[0] = col(2).astype(BF16)
    hq_ref[0] = _silu(col(3))
    logits = lbl_ref[...]
    e = jnp.exp(logits - jnp.max(logits, 0, keepdims=True))
    lb = jnp.sum(e[0:layer + 1], 0, keepdims=True) / jnp.sum(e, 0, keepdims=True)
    for src, k_ref, g_ref in ((4, kf_ref, gf_ref), (5, kb_ref, gb_ref)):
        f = lb + (1.0 - lb) * jax.nn.sigmoid(col(src))
        k_ref[0] = 1.0 - f
        g_ref[0] = jnp.log(f)
    hv_ref[0] = col(6)
    hg_ref[0] = _silu(col(7))


def _hybrid_in(x, mod, w, lb_logits, layer, tm):
    bsz, seq, d = x.shape
    tok = pl.BlockSpec((1, tm, d), lambda i, j: (i, j, 0))
    slab = pl.BlockSpec((1, tm, B_W), lambda i, j: (i, j, 0))
    sds = lambda dt: jax.ShapeDtypeStruct((bsz, seq, B_W), dt)
    return pl.pallas_call(
        functools.partial(_hyb_in_body, layer),
        grid=(bsz, seq // tm),
        in_specs=[tok, pl.BlockSpec((1, 3, d), lambda i, j: (i, 0, 0)),
                  _resident(w.shape), _resident(lb_logits.shape)],
        out_specs=[slab] * 10,
        out_shape=[sds(BF16)] * 3 + [sds(F32)] * 7,
        compiler_params=_params("parallel", "parallel"),
        name="hybrid_in",
    )(x, mod, w, lb_logits)


def _win_body(seq, sink_ref, q_ref, km_ref, k0_ref, kp_ref, vm_ref, v0_ref, vp_ref,
              pc_ref, pm_ref, p0_ref, pp_ref, o_ref):
    i = pl.program_id(1)
    blk = A_BLOCK
    gw = A_GROUP * A_HEAD_DIM
    q = q_ref[0]
    k3 = jnp.concatenate([km_ref[0], k0_ref[0], kp_ref[0]], axis=0)
    v3 = jnp.concatenate([vm_ref[0], v0_ref[0], vp_ref[0]], axis=0)
    pk = jnp.concatenate([pm_ref[0], p0_ref[0], pp_ref[0]], axis=1)
    dist = jnp.abs(pc_ref[0] - pk).astype(F32)
    qi = lax.broadcasted_iota(jnp.int32, (blk, 3 * blk), 0) + blk
    ki = lax.broadcasted_iota(jnp.int32, (blk, 3 * blk), 1)
    abs_k = i * blk + ki - blk
    valid = (jnp.abs(qi - ki) <= WINDOW) & (abs_k >= 0) & (abs_k < seq)
    head_of_lane = lax.broadcasted_iota(jnp.int32, (blk, gw), 1) // A_HEAD_DIM
    for j in range(A_KV_HEADS):
        qg = q[:, j * gw:(j + 1) * gw]
        kt = k3[:, j * gw:(j + 1) * gw]
        vt = v3[:, j * gw:(j + 1) * gw]
        qs = jnp.concatenate([jnp.where(head_of_lane == g, qg, jnp.zeros_like(qg))
                              for g in range(A_GROUP)], axis=0)
        s = lax.dot_general(qs, kt, _NT, preferred_element_type=F32)
        probs = []
        for g in range(A_GROUP):
            hd = j * A_GROUP + g
            slope = 2.0 ** (-8.0 * (hd + 1) / A_HEADS)
            sg = s[g * blk:(g + 1) * blk] - slope * dist
            sg = jnp.where(valid, sg, -jnp.inf)
            sk = sink_ref[hd]
            m = jnp.maximum(jnp.max(sg, -1, keepdims=True), sk)
            p = jnp.exp(sg - m)
            p = p / (jnp.sum(p, -1, keepdims=True) + jnp.exp(sk - m))
            probs.append(p.astype(BF16))
        o = _dot(jnp.concatenate(probs, axis=0), vt)
        acc = jnp.zeros((blk, gw), F32)
        for g in range(A_GROUP):
            acc = acc + jnp.where(head_of_lane == g, o[g * blk:(g + 1) * blk], 0.0)
        o_ref[0, :, j * gw:(j + 1) * gw] = acc.astype(BF16)


def _window_attention(aq, ak, av, pos, sink):
    bsz, seq, w = aq.shape
    nb = seq // A_BLOCK
    cur = lambda i, j: (i, j, 0)
    prv = lambda i, j: (i, jnp.maximum(j - 1, 0), 0)
    nxt = lambda i, j: (i, jnp.minimum(j + 1, nb - 1), 0)
    tok = lambda f: pl.BlockSpec((1, A_BLOCK, w), f)
    row = lambda f: pl.BlockSpec((1, 1, A_BLOCK), lambda i, j: (f(i, j)[0], 0, f(i, j)[1]))
    return pl.pallas_call(
        functools.partial(_win_body, seq),
        grid=(bsz, nb),
        in_specs=[pl.BlockSpec(memory_space=pltpu.SMEM),
                  tok(cur), tok(prv), tok(cur), tok(nxt), tok(prv), tok(cur), tok(nxt),
                  pl.BlockSpec((1, A_BLOCK, 1), cur), row(prv), row(cur), row(nxt)],
        out_specs=tok(cur),
        out_shape=jax.ShapeDtypeStruct((bsz, seq, w), BF16),
        compiler_params=_params("parallel", "parallel"),
        name="window_attention",
    )(sink, aq, ak, ak, ak, av, av, av,
      pos.reshape(bsz, seq, 1), pos.reshape(bsz, 1, seq), pos.reshape(bsz, 1, seq), pos.reshape(bsz, 1, seq))


def _cumsum_rows(tri, logf):
    hi = logf.astype(BF16)
    lo = (logf - hi.astype(F32)).astype(BF16)
    return _dot(tri, hi) + _dot(tri, lo)


def _hgrn_chunk(rows, tri_cum, tri_keep, mid, end, q_ref, k_ref, g_ref, v_ref, st_ref, o_ref):
    b_all = _cumsum_rows(tri_cum, g_ref[0, rows, :])
    for h in range(B_HEADS):
        lanes = slice(h * B_KEY_DIM, (h + 1) * B_KEY_DIM)
        b = b_all[:, lanes]
        b_mid = b[mid:mid + 1]
        b_end = b[end:end + 1]
        q = q_ref[0, rows, lanes]
        k = k_ref[0, rows, lanes]
        v = v_ref[0, rows, lanes].astype(BF16)
        a = lax.dot_general((q * jnp.exp(b - b_mid)).astype(BF16), (k * jnp.exp(b_mid - b)).astype(BF16),
                            _NT, preferred_element_type=F32)
        o = _dot(jnp.where(tri_keep, a, 0.0).astype(BF16), v)
        st = st_ref[h]
        o = o + lax.dot_general((q * jnp.exp(b)).astype(BF16), st.astype(BF16), _NT,
                                preferred_element_type=F32)
        o_ref[0, rows, lanes] = o
        d_st = lax.dot_general(v, (k * jnp.exp(b_end - b)).astype(BF16), _TN, preferred_element_type=F32)
        st_ref[h] = st * jnp.exp(b_end) + d_st


def _hgrn_body(n_chunks, qf_ref, kf_ref, gf_ref, vf_ref, qb_ref, kb_ref, gb_ref, vb_ref,
               of_ref, ob_ref, sf_ref, sb_ref):
    @pl.when(pl.program_id(1) == 0)
    def _():
        sf_ref[...] = jnp.zeros_like(sf_ref)
        sb_ref[...] = jnp.zeros_like(sb_ref)

    r = lax.broadcasted_iota(jnp.int32, (B_CHUNK, B_CHUNK), 0)
    c = lax.broadcasted_iota(jnp.int32, (B_CHUNK, B_CHUNK), 1)
    lower = r >= c
    upper = r <= c
    lower_ones = lower.astype(BF16)
    upper_ones = upper.astype(BF16)

    for t in range(n_chunks):
        rows_f = slice(t * B_CHUNK, (t + 1) * B_CHUNK)
        _hgrn_chunk(rows_f, lower_ones, lower, B_CHUNK // 2 - 1, B_CHUNK - 1,
                    qf_ref, kf_ref, gf_ref, vf_ref, sf_ref, of_ref)
        rows_b = slice((n_chunks - 1 - t) * B_CHUNK, (n_chunks - t) * B_CHUNK)
        _hgrn_chunk(rows_b, upper_ones, upper, B_CHUNK // 2, 0,
                    qb_ref, kb_ref, gb_ref, vb_ref, sb_ref, ob_ref)


def _hgrn2(hq, kf, gf, kb, gb, hv, tb):
    bsz, seq, w = hq.shape
    n = seq // tb
    fwd = pl.BlockSpec((1, tb, w), lambda i, j: (i, j, 0))
    bwd = pl.BlockSpec((1, tb, w), lambda i, j: (i, n - 1 - j, 0))
    out = jax.ShapeDtypeStruct((bsz, seq, w), F32)
    state = pltpu.VMEM((B_HEADS, B_VAL_DIM, B_KEY_DIM), F32)
    return pl.pallas_call(
        functools.partial(_hgrn_body, tb // B_CHUNK),
        grid=(bsz, n),
        in_specs=[fwd] * 4 + [bwd] * 4,
        out_specs=[fwd, bwd],
        out_shape=[out, out],
        scratch_shapes=[state, state],
        compiler_params=_params("parallel", "arbitrary"),
        name="hgrn2_scan",
    )(hq, kf, gf, hv, hq, kb, gb, hv)


def _hyb_out_body(x_ref, mod_ref, oa_ref, of_ref, ob_ref, hg_ref, nw_ref, w_ref, g_ref, b_ref, o_ref):
    o = of_ref[0] + ob_ref[0]
    heads = []
    for h in range(B_HEADS):
        oh = o[:, h * B_VAL_DIM:(h + 1) * B_VAL_DIM]
        heads.append(oh * lax.rsqrt(jnp.mean(oh * oh, -1, keepdims=True) + RMS_EPS))
    ohg = (jnp.concatenate(heads, axis=1) * nw_ref[...] * hg_ref[0]).astype(BF16)
    y = _dot(oa_ref[0], w_ref[0:A_Q_W, :]) + _dot(ohg, w_ref[A_Q_W:, :])
    o_ref[0] = _residual_ln(x_ref[0], y, mod_ref, g_ref, b_ref, 1.0)


def _hybrid_out(x, mod, oa, of, ob, hg, nw, w, g, b, tm):
    bsz, seq, d = x.shape
    tok = pl.BlockSpec((1, tm, d), lambda i, j: (i, j, 0))
    slab = pl.BlockSpec((1, tm, B_W), lambda i, j: (i, j, 0))
    return pl.pallas_call(
        _hyb_out_body,
        grid=(bsz, seq // tm),
        in_specs=[tok, pl.BlockSpec((1, 3, d), lambda i, j: (i, 0, 0)), slab, slab, slab, slab,
                  _resident(nw.shape), _resident(w.shape), _resident(g.shape), _resident(b.shape)],
        out_specs=tok,
        out_shape=jax.ShapeDtypeStruct(x.shape, F32),
        compiler_params=_params("parallel", "parallel"),
        name="hybrid_out",
    )(x, mod, oa, of, ob, hg, nw, w, g, b)


def _rms(x, w_ref):
    return (x * lax.rsqrt(jnp.mean(x * x, -1, keepdims=True) + RMS_EPS) * w_ref[...]).astype(BF16)


def _mla_in_body(x_ref, mod_ref, pos_ref, freq_ref, sgn_ref, wd_ref, qn_ref, kvn_ref, wq_ref, wkv_ref,
                 q_ref, k_ref, v_ref):
    h = _modulate(x_ref[0], mod_ref).astype(BF16)
    down = _dot(h, wd_ref[...])
    ang = pos_ref[0].astype(F32) * freq_ref[...]
    cos = jnp.cos(ang)
    sin = jnp.sin(ang) * sgn_ref[...]
    off = C_Q_RANK + C_KV_RANK
    k_rope = down[:, off:off + LANES] * cos + down[:, off + LANES:off + 2 * LANES] * sin
    cq = _rms(down[:, :C_Q_RANK], qn_ref)
    ckv = _rms(down[:, C_Q_RANK:off], kvn_ref)
    qq = _dot(cq, wq_ref[...])
    kv = _dot(ckv, wkv_ref[...])
    scale = (C_NOPE + C_ROPE) ** -0.5 * LOG2_E
    nh = C_HEADS * LANES
    for hd in range(C_HEADS):
        lanes = slice(hd * LANES, (hd + 1) * LANES)
        q_ref[0, hd] = ((qq[:, lanes] * cos + qq[:, nh + hd * LANES:nh + (hd + 1) * LANES] * sin) * scale).astype(BF16)
        k_ref[0, hd] = (kv[:, lanes] + k_rope).astype(BF16)
    low = lax.broadcasted_iota(jnp.int32, (x_ref.shape[1], LANES), 1) < C_V
    for pr in range(C_HEADS // 2):
        pair = kv[:, nh + pr * LANES:nh + (pr + 1) * LANES]
        v_ref[0, 2 * pr] = jnp.where(low, pair, 1.0).astype(BF16)
        v_ref[0, 2 * pr + 1] = jnp.where(low, 1.0, pair).astype(BF16)


def _mla_in(x, mod, pos, freq, sgn, wd, qn, kvn, wq, wkv, tm):
    bsz, seq, d = x.shape
    tok = pl.BlockSpec((1, tm, d), lambda i, j: (i, j, 0))
    heads = lambda n: pl.BlockSpec((1, n, tm, LANES), lambda i, j: (i, 0, j, 0))
    sds = lambda n: jax.ShapeDtypeStruct((bsz, n, seq, LANES), BF16)
    return pl.pallas_call(
        _mla_in_body,
        grid=(bsz, seq // tm),
        in_specs=[tok, pl.BlockSpec((1, 3, d), lambda i, j: (i, 0, 0)),
                  pl.BlockSpec((1, tm, 1), lambda i, j: (i, j, 0)),
                  _resident(freq.shape), _resident(sgn.shape), _resident(wd.shape),
                  _resident(qn.shape), _resident(kvn.shape), _resident(wq.shape), _resident(wkv.shape)],
        out_specs=[heads(C_HEADS)] * 3,
        out_shape=[sds(C_HEADS)] * 3,
        compiler_params=_params("parallel", "parallel"),
        name="mla_in",
    )(x, mod, pos.reshape(bsz, seq, 1), freq, sgn, wd, qn, kvn, wq, wkv)


def _mla_scores(q, k, s_ref, m_ref):
    s = lax.dot_general(q, k, _NT, preferred_element_type=F32)
    m_ref[...] = jnp.max(s, -1, keepdims=True)
    s_ref[...] = s


def _mla_finish(s_ref, m_ref, v):
    p = jnp.exp2(s_ref[...] - m_ref[...]).astype(BF16)
    o = _dot(p, v)
    return o / pltpu.roll(o, C_V, 1)


def _mla_attn_body(q_ref, k_ref, vp_ref, vc_ref, o_ref, s0_ref, s1_ref, m0_ref, m1_ref, o0_ref):
    @pl.when(pl.program_id(0) == 0)
    def _():
        s1_ref[...] = jnp.zeros_like(s1_ref)
        m1_ref[...] = jnp.zeros_like(m1_ref)
        o0_ref[...] = jnp.zeros_like(o0_ref)

    _mla_scores(q_ref[0, 0], k_ref[0, 0], s0_ref, m0_ref)
    o1 = _mla_finish(s1_ref, m1_ref, vp_ref[0, 0])
    lane = lax.broadcasted_iota(jnp.int32, o1.shape, 1)
    o_ref[0] = jnp.where(lane < C_V, o0_ref[...], o1).astype(BF16)

    _mla_scores(q_ref[0, 1], k_ref[0, 1], s1_ref, m1_ref)
    o0_ref[...] = _mla_finish(s0_ref, m0_ref, vc_ref[0, 0])


def _mla_attention(q, k, v, tq):
    bsz, nh, seq, _ = q.shape
    nq = seq // tq
    n_items = bsz * (nh // 2) * nq

    def item(t):
        t = jnp.minimum(t, n_items - 1)
        return t // (nq * (nh // 2)), (t // nq) % (nh // 2), t % nq

    def prev(t):
        return item(jnp.maximum(t - 1, 0))

    return pl.pallas_call(
        _mla_attn_body,
        grid=(n_items + 1,),
        in_specs=[pl.BlockSpec((1, 2, tq, LANES), lambda t: (item(t)[0], item(t)[1], item(t)[2], 0)),
                  pl.BlockSpec((1, 2, seq, LANES), lambda t: (item(t)[0], item(t)[1], 0, 0)),
                  pl.BlockSpec((1, 1, seq, LANES), lambda t: (prev(t)[0], 2 * prev(t)[1] + 1, 0, 0)),
                  pl.BlockSpec((1, 1, seq, LANES), lambda t: (item(t)[0], 2 * item(t)[1], 0, 0))],
        out_specs=pl.BlockSpec((1, tq, LANES), lambda t: (prev(t)[0], prev(t)[2], prev(t)[1])),
        out_shape=jax.ShapeDtypeStruct((bsz, seq, nh * C_V), BF16),
        scratch_shapes=[pltpu.VMEM((tq, seq), F32), pltpu.VMEM((tq, seq), F32),
                        pltpu.VMEM((tq, 1), F32), pltpu.VMEM((tq, 1), F32), pltpu.VMEM((tq, LANES), F32)],
        compiler_params=_params("arbitrary"),
        name="mla_attention",
    )(q, k, v, v)


def _proj_out_body(x_ref, mod_ref, o_ref_in, w_ref, g_ref, b_ref, o_ref):
    y = _dot(o_ref_in[0], w_ref[...])
    o_ref[0] = _residual_ln(x_ref[0], y, mod_ref, g_ref, b_ref, 1.0)


def _proj_out(x, mod, o, w, g, b, tm):
    bsz, seq, d = x.shape
    tok = pl.BlockSpec((1, tm, d), lambda i, j: (i, j, 0))
    return pl.pallas_call(
        _proj_out_body,
        grid=(bsz, seq // tm),
        in_specs=[tok, pl.BlockSpec((1, 3, d), lambda i, j: (i, 0, 0)),
                  pl.BlockSpec((1, tm, o.shape[-1]), lambda i, j: (i, j, 0)),
                  _resident(w.shape), _resident(g.shape), _resident(b.shape)],
        out_specs=tok,
        out_shape=jax.ShapeDtypeStruct(x.shape, F32),
        compiler_params=_params("parallel", "parallel"),
        name="mla_out",
    )(x, mod, o, w, g, b)


def _hybrid_w_in(w):
    d = w.shape[0]
    aq = w[:, :A_Q_W]
    ak = w[:, A_Q_W:A_Q_W + A_KV_W].reshape(d, A_KV_HEADS, 1, A_HEAD_DIM)
    av = w[:, A_Q_W + A_KV_W:A_Q_W + 2 * A_KV_W].reshape(d, A_KV_HEADS, 1, A_HEAD_DIM)
    tile = lambda t: jnp.broadcast_to(t, (d, A_KV_HEADS, A_GROUP, A_HEAD_DIM)).reshape(d, A_Q_W)
    return jnp.concatenate([aq, tile(ak), tile(av), w[:, A_Q_W + 2 * A_KV_W:]], axis=1).astype(BF16)


def _swap_halves(t):
    half = t.shape[-1] // 2
    return jnp.concatenate([t[..., half:], t[..., :half]], axis=-1)


def _mla_weights(w_down, w_uq, w_ukv):
    d = w_down.shape[0]
    off = C_Q_RANK + C_KV_RANK
    pad_rope = lambda t: jnp.pad(t, [(0, 0)] * (t.ndim - 1) + [(C_NOPE, LANES - C_NOPE - C_ROPE)])
    kr = w_down[:, off:]
    wd = jnp.concatenate([w_down[:, :off], pad_rope(kr), pad_rope(_swap_halves(kr))], axis=1)
    uq = w_uq.reshape(C_Q_RANK, C_HEADS, C_NOPE + C_ROPE)
    q_main = jnp.pad(uq, [(0, 0), (0, 0), (0, LANES - C_NOPE - C_ROPE)])
    q_swap = pad_rope(_swap_halves(uq[..., C_NOPE:]))
    wq = jnp.concatenate([q_main.reshape(C_Q_RANK, -1), q_swap.reshape(C_Q_RANK, -1)], axis=1)
    ukv = w_ukv.reshape(C_KV_RANK, C_HEADS, C_NOPE + C_V)
    k_main = jnp.pad(ukv[..., :C_NOPE], [(0, 0), (0, 0), (0, LANES - C_NOPE)])
    wkv = jnp.concatenate([k_main.reshape(C_KV_RANK, -1), ukv[..., C_NOPE:].reshape(C_KV_RANK, -1)], axis=1)
    return wd.astype(BF16), wq.astype(BF16), wkv.astype(BF16)


def _rope_tables():
    half = C_ROPE // 2
    freqs = ROPE_THETA ** (-jnp.arange(half, dtype=F32) / half)
    zeros = jnp.zeros((C_NOPE,), F32)
    tail = jnp.zeros((LANES - C_NOPE - C_ROPE,), F32)
    freq = jnp.concatenate([zeros, freqs, freqs, tail]).reshape(1, LANES)
    ones = jnp.ones((half,), F32)
    sgn = jnp.concatenate([zeros, -ones, ones, tail]).reshape(1, LANES)
    return freq, sgn


def kernel(x, c, positions, ada_w, ada_b, ln_g, ln_b, ffn_w_gate, ffn_w_up, ffn_w_down,
           hyb_w_in, hyb_w_out, attn_sink, hgrn_lb_logits, hgrn_norm_w,
           mla_w_down, mla_q_norm, mla_kv_norm, mla_w_uq, mla_w_ukv, mla_w_out):
    bsz, seq, d = x.shape
    tm = min(512, seq)
    mod_all = _modulation(c, ada_w, ada_b).reshape(DEPTH, bsz, N_SUB, 3, d)
    row = lambda t: t.reshape(1, -1)
    freq, sgn = _rope_tables()
    for layer in range(DEPTH):
        mod = lambda j, l=layer: mod_all[l, :, j]
        ffn = lambda xx, j, s, l=layer: _ffn_sublayer(
            xx, mod(s), ffn_w_gate[l, j].astype(BF16), ffn_w_up[l, j].astype(BF16),
            ffn_w_down[l, j].astype(BF16), row(ln_g[l, s]), row(ln_b[l, s]), min(1024, seq))
        x = ffn(x, 0, 0)
        if layer % 2 == 0:
            e = layer // 2
            aq, ak, av, hq, kf, gf, kb, gb, hv, hg = _hybrid_in(
                x, mod(1), _hybrid_w_in(hyb_w_in[e]), hgrn_lb_logits.astype(F32), layer, tm)
            oa = _window_attention(aq, ak, av, positions, attn_sink[e].astype(F32))
            of, ob = _hgrn2(hq, kf, gf, kb, gb, hv, tm)
            x = _hybrid_out(x, mod(1), oa, of, ob, hg, row(hgrn_norm_w[e]), hyb_w_out[e].astype(BF16),
                            row(ln_g[layer, 1]), row(ln_b[layer, 1]), tm)
        else:
            o = layer // 2
            wd, wq, wkv = _mla_weights(mla_w_down[o], mla_w_uq[o], mla_w_ukv[o])
            q, k, v = _mla_in(x, mod(1), positions, freq, sgn, wd, row(mla_q_norm[o]), row(mla_kv_norm[o]),
                              wq, wkv, tm)
            att = _mla_attention(q, k, v, tm)
            x = _proj_out(x, mod(1), att, mla_w_out[o].astype(BF16), row(ln_g[layer, 1]), row(ln_b[layer, 1]), tm)
        x = ffn(x, 1, 2)
    return x
```

```python
import functools

import jax
import jax.numpy as jnp
from jax import lax
from jax.experimental import pallas as pl
from jax.experimental.pallas import tpu as pltpu

F32 = jnp.float32
BF16 = jnp.bfloat16

D_MODEL = 1024
DEPTH = 2
D_FF = 2816
A_HEADS = 8
A_KV_HEADS = 2
A_HEAD_DIM = 64
A_GROUP = A_HEADS // A_KV_HEADS
WINDOW = 128
A_BLOCK = 128
B_HEADS = 4
B_KEY_DIM = 128
B_VAL_DIM = 128
B_CHUNK = 64
C_HEADS = 16
C_Q_RANK = 256
C_KV_RANK = 256
C_NOPE = 64
C_ROPE = 32
C_V = 64
ROPE_THETA = 10000.0
LN_EPS = 1e-5
RMS_EPS = 1e-6
LOG2_E = 1.4426950408889634
ALPHA = (2 * DEPTH) ** 0.25
N_SUB = 3
A_Q_W = A_HEADS * A_HEAD_DIM
A_KV_W = A_KV_HEADS * A_HEAD_DIM
B_W = B_HEADS * B_KEY_DIM
LANES = 128
VMEM_LIMIT = 56 * 1024 * 1024

_NT = (((1,), (1,)), ((), ()))
_TN = (((0,), (0,)), ((), ()))


def _params(*sem):
    return pltpu.CompilerParams(dimension_semantics=sem, vmem_limit_bytes=VMEM_LIMIT)


def _resident(shape):
    zeros = (0,) * len(shape)
    return pl.BlockSpec(shape, lambda *_: zeros, pipeline_mode=pl.Buffered(1))


def _dot(a, b):
    return jnp.dot(a, b, preferred_element_type=F32)


def _modulate(x, mod_ref):
    return x * (1.0 + mod_ref[0, 1:2, :]) + mod_ref[0, 0:1, :]


def _residual_ln(x, y, mod_ref, g_ref, b_ref, res_w):
    r = ALPHA * x + res_w * (1.0 + mod_ref[0, 2:3, :]) * y
    mu = jnp.mean(r, -1, keepdims=True)
    d = r - mu
    var = jnp.mean(d * d, -1, keepdims=True)
    return d * lax.rsqrt(var + LN_EPS) * g_ref[...] + b_ref[...]


def _silu(v):
    return v * jax.nn.sigmoid(v)


def _mod_body(c_ref, w_ref, b_ref, o_ref):
    cond = _silu(c_ref[...])
    o_ref[0] = jnp.dot(cond, w_ref[0], preferred_element_type=F32,
                       precision=lax.Precision.HIGHEST) + b_ref[0]


def _modulation(c, ada_w, ada_b):
    bsz, d = c.shape
    n = ada_w.shape[-1]
    tn = 1152
    return pl.pallas_call(
        _mod_body,
        grid=(DEPTH, n // tn),
        in_specs=[pl.BlockSpec((bsz, d), lambda l, j: (0, 0)),
                  pl.BlockSpec((1, d, tn), lambda l, j: (l, 0, j)),
                  pl.BlockSpec((1, 1, tn), lambda l, j: (l, 0, j))],
        out_specs=pl.BlockSpec((1, bsz, tn), lambda l, j: (l, 0, j)),
        out_shape=jax.ShapeDtypeStruct((DEPTH, bsz, n), F32),
        compiler_params=_params("parallel", "parallel"),
        name="adaln_mod",
    )(c, ada_w, ada_b.reshape(DEPTH, 1, n))


def _ffn_rows(x, mod_ref, wg_ref, wu_ref, wd_ref, g_ref, b_ref):
    h = _modulate(x, mod_ref).astype(BF16)
    gate = _dot(h, wg_ref[...])
    up = _dot(h, wu_ref[...])
    act = (_silu(gate) * up).astype(BF16)
    y = _dot(act, wd_ref[...])
    return _residual_ln(x, y, mod_ref, g_ref, b_ref, 0.5)


def _ffn_body(x_ref, mod_ref, *rest):
    o_ref = rest[-1]
    o_ref[0] = _ffn_rows(x_ref[0], mod_ref, *rest[:-1])


def _ffn_specs(d, ffn):
    return [pl.BlockSpec((1, 3, d), lambda i, j: (i, 0, 0))] + [_resident(t.shape) for t in ffn[1:]]


def _ffn_sublayer(x, ffn, tm):
    bsz, seq, d = x.shape
    tok = pl.BlockSpec((1, tm, d), lambda i, j: (i, j, 0))
    return pl.pallas_call(
        _ffn_body,
        grid=(bsz, seq // tm),
        in_specs=[tok] + _ffn_specs(d, ffn),
        out_specs=tok,
        out_shape=jax.ShapeDtypeStruct(x.shape, F32),
        compiler_params=_params("parallel", "parallel"),
        name="ffn_sublayer",
    )(x, *ffn)


def _hyb_in_body(layer, x_ref, mod_ref, w_ref, lbl_ref,
                 aq_ref, ak_ref, av_ref, hq_ref, kf_ref, gf_ref, kb_ref, gb_ref, hv_ref, hg_ref):
    h = _modulate(x_ref[0], mod_ref).astype(BF16)
    p = _dot(h, w_ref[...])
    col = lambda j: p[:, j * B_W:(j + 1) * B_W]
    aq_ref[0] = (col(0) * (A_HEAD_DIM ** -0.5 * LOG2_E)).astype(BF16)
    ak_ref[0] = col(1).astype(BF16)
    av_ref[0] = col(2).astype(BF16)
    hq_ref[0] = _silu(col(3))
    logits = lbl_ref[...]
    e = jnp.exp(logits - jnp.max(logits, 0, keepdims=True))
    lb = jnp.sum(e[0:layer + 1], 0, keepdims=True) / jnp.sum(e, 0, keepdims=True)
    for src, k_ref, g_ref in ((4, kf_ref, gf_ref), (5, kb_ref, gb_ref)):
        f = lb + (1.0 - lb) * jax.nn.sigmoid(col(src))
        k_ref[0] = 1.0 - f
        g_ref[0] = jnp.log(f)
    hv_ref[0] = col(6)
    hg_ref[0] = _silu(col(7))


def _hybrid_in(x, mod, w, lb_logits, layer, tm):
    bsz, seq, d = x.shape
    tok = pl.BlockSpec((1, tm, d), lambda i, j: (i, j, 0))
    slab = pl.BlockSpec((1, tm, B_W), lambda i, j: (i, j, 0))
    sds = lambda dt: jax.ShapeDtypeStruct((bsz, seq, B_W), dt)
    return pl.pallas_call(
        functools.partial(_hyb_in_body, layer),
        grid=(bsz, seq // tm),
        in_specs=[tok, pl.BlockSpec((1, 3, d), lambda i, j: (i, 0, 0)),
                  _resident(w.shape), _resident(lb_logits.shape)],
        out_specs=[slab] * 10,
        out_shape=[sds(BF16)] * 3 + [sds(F32)] * 7,
        compiler_params=_params("parallel", "parallel"),
        name="hybrid_in",
    )(x, mod, w, lb_logits)


def _win_body(seq, nblk, sink_ref, q_ref, km_ref, k0_ref, kp_ref, vm_ref, v0_ref, vp_ref,
              pc_ref, pm_ref, p0_ref, pp_ref, o_ref):
    blk = A_BLOCK
    gw = A_GROUP * A_HEAD_DIM
    k_all = jnp.concatenate([km_ref[0], k0_ref[0], kp_ref[0]], axis=0)
    v_all = jnp.concatenate([vm_ref[0], v0_ref[0], vp_ref[0]], axis=0)
    pk_all = jnp.concatenate([pm_ref[0], p0_ref[0], pp_ref[0]], axis=1)
    qi = lax.broadcasted_iota(jnp.int32, (blk, 3 * blk), 0) + blk
    ki = lax.broadcasted_iota(jnp.int32, (blk, 3 * blk), 1)
    in_window = jnp.abs(qi - ki) <= WINDOW
    head_of_lane = lax.broadcasted_iota(jnp.int32, (blk, gw), 1) // A_HEAD_DIM
    for r in range(nblk):
        i = pl.program_id(1) * nblk + r
        rows = slice(r * blk, (r + 1) * blk)
        band = slice(r * blk, (r + 3) * blk)
        dist = jnp.abs(pc_ref[0, rows, :] - pk_all[:, band]).astype(F32)
        abs_k = i * blk + ki - blk
        valid = in_window & (abs_k >= 0) & (abs_k < seq)
        for j in range(A_KV_HEADS):
            lanes = slice(j * gw, (j + 1) * gw)
            qg = q_ref[0, rows, lanes]
            qs = jnp.concatenate([jnp.where(head_of_lane == g, qg, jnp.zeros_like(qg))
                                  for g in range(A_GROUP)], axis=0)
            s = lax.dot_general(qs, k_all[band, lanes], _NT, preferred_element_type=F32)
            probs = []
            for g in range(A_GROUP):
                hd = j * A_GROUP + g
                slope = 2.0 ** (-8.0 * (hd + 1) / A_HEADS) * LOG2_E
                sg = s[g * blk:(g + 1) * blk] - slope * dist
                sg = jnp.where(valid, sg, -jnp.inf)
                sk = sink_ref[hd] * LOG2_E
                m = jnp.maximum(jnp.max(sg, -1, keepdims=True), sk)
                p = jnp.exp2(sg - m)
                p = p / (jnp.sum(p, -1, keepdims=True) + jnp.exp2(sk - m))
                probs.append(p.astype(BF16))
            o = _dot(jnp.concatenate(probs, axis=0), v_all[band, lanes])
            acc = jnp.zeros((blk, gw), F32)
            for g in range(A_GROUP):
                acc = acc + jnp.where(head_of_lane == g, o[g * blk:(g + 1) * blk], 0.0)
            o_ref[0, rows, lanes] = acc.astype(BF16)


def _window_attention(aq, ak, av, pos, sink):
    bsz, seq, w = aq.shape
    nb = seq // A_BLOCK
    nblk = min(4, nb)
    cur = lambda i, j: (i, j, 0)
    prv = lambda i, j: (i, jnp.maximum(j * nblk - 1, 0), 0)
    nxt = lambda i, j: (i, jnp.minimum((j + 1) * nblk, nb - 1), 0)
    edge = lambda f: pl.BlockSpec((1, A_BLOCK, w), f)
    mid = pl.BlockSpec((1, nblk * A_BLOCK, w), cur)
    row = lambda n, f: pl.BlockSpec((1, 1, n), lambda i, j: (f(i, j)[0], 0, f(i, j)[1]))
    pos_row = pos.reshape(bsz, 1, seq)
    return pl.pallas_call(
        functools.partial(_win_body, seq, nblk),
        grid=(bsz, nb // nblk),
        in_specs=[pl.BlockSpec(memory_space=pltpu.SMEM),
                  mid, edge(prv), mid, edge(nxt), edge(prv), mid, edge(nxt),
                  pl.BlockSpec((1, nblk * A_BLOCK, 1), cur),
                  row(A_BLOCK, prv), row(nblk * A_BLOCK, cur), row(A_BLOCK, nxt)],
        out_specs=mid,
        out_shape=jax.ShapeDtypeStruct((bsz, seq, w), BF16),
        compiler_params=_params("parallel", "parallel"),
        name="window_attention",
    )(sink, aq, ak, ak, ak, av, av, av, pos.reshape(bsz, seq, 1), pos_row, pos_row, pos_row)


def _cumsum_rows(tri, logf):
    hi = logf.astype(BF16)
    lo = (logf - hi.astype(F32)).astype(BF16)
    return _dot(tri, hi) + _dot(tri, lo)


def _hgrn_chunk(rows, tri_cum, tri_keep, mid, end, q_ref, k_ref, g_ref, v_ref, st_ref, o_ref):
    b_all = _cumsum_rows(tri_cum, g_ref[0, rows, :])
    for h in range(B_HEADS):
        lanes = slice(h * B_KEY_DIM, (h + 1) * B_KEY_DIM)
        b = b_all[:, lanes]
        b_mid = b[mid:mid + 1]
        b_end = b[end:end + 1]
        q = q_ref[0, rows, lanes]
        k = k_ref[0, rows, lanes]
        v = v_ref[0, rows, lanes].astype(BF16)
        a = lax.dot_general((q * jnp.exp(b - b_mid)).astype(BF16), (k * jnp.exp(b_mid - b)).astype(BF16),
                            _NT, preferred_element_type=F32)
        o = _dot(jnp.where(tri_keep, a, 0.0).astype(BF16), v)
        st = st_ref[h]
        o = o + lax.dot_general((q * jnp.exp(b)).astype(BF16), st.astype(BF16), _NT,
                                preferred_element_type=F32)
        o_ref[0, rows, lanes] = o
        d_st = lax.dot_general(v, (k * jnp.exp(b_end - b)).astype(BF16), _TN, preferred_element_type=F32)
        st_ref[h] = st * jnp.exp(b_end) + d_st


def _hgrn_body(n_chunks, qf_ref, kf_ref, gf_ref, vf_ref, qb_ref, kb_ref, gb_ref, vb_ref,
               of_ref, ob_ref, sf_ref, sb_ref):
    @pl.when(pl.program_id(1) == 0)
    def _():
        sf_ref[...] = jnp.zeros_like(sf_ref)
        sb_ref[...] = jnp.zeros_like(sb_ref)

    r = lax.broadcasted_iota(jnp.int32, (B_CHUNK, B_CHUNK), 0)
    c = lax.broadcasted_iota(jnp.int32, (B_CHUNK, B_CHUNK), 1)
    lower = r >= c
    upper = r <= c
    lower_ones = lower.astype(BF16)
    upper_ones = upper.astype(BF16)

    for t in range(n_chunks):
        rows_f = slice(t * B_CHUNK, (t + 1) * B_CHUNK)
        _hgrn_chunk(rows_f, lower_ones, lower, B_CHUNK // 2 - 1, B_CHUNK - 1,
                    qf_ref, kf_ref, gf_ref, vf_ref, sf_ref, of_ref)
        rows_b = slice((n_chunks - 1 - t) * B_CHUNK, (n_chunks - t) * B_CHUNK)
        _hgrn_chunk(rows_b, upper_ones, upper, B_CHUNK // 2, 0,
                    qb_ref, kb_ref, gb_ref, vb_ref, sb_ref, ob_ref)


def _hgrn2(hq, kf, gf, kb, gb, hv, tb):
    bsz, seq, w = hq.shape
    n = seq // tb
    fwd = pl.BlockSpec((1, tb, w), lambda i, j: (i, j, 0))
    bwd = pl.BlockSpec((1, tb, w), lambda i, j: (i, n - 1 - j, 0))
    out = jax.ShapeDtypeStruct((bsz, seq, w), F32)
    state = pltpu.VMEM((B_HEADS, B_VAL_DIM, B_KEY_DIM), F32)
    return pl.pallas_call(
        functools.partial(_hgrn_body, tb // B_CHUNK),
        grid=(bsz, n),
        in_specs=[fwd] * 4 + [bwd] * 4,
        out_specs=[fwd, bwd],
        out_shape=[out, out],
        scratch_shapes=[state, state],
        compiler_params=_params("parallel", "arbitrary"),
        name="hgrn2_scan",
    )(hq, kf, gf, hv, hq, kb, gb, hv)


def _hyb_out_body(x_ref, mod_ref, oa_ref, of_ref, ob_ref, hg_ref, nw_ref, w_ref, g_ref, b_ref, *ffn_and_out):
    o = of_ref[0] + ob_ref[0]
    heads = []
    for h in range(B_HEADS):
        oh = o[:, h * B_VAL_DIM:(h + 1) * B_VAL_DIM]
        heads.append(oh * lax.rsqrt(jnp.mean(oh * oh, -1, keepdims=True) + RMS_EPS))
    ohg = (jnp.concatenate(heads, axis=1) * nw_ref[...] * hg_ref[0]).astype(BF16)
    y = _dot(oa_ref[0], w_ref[0:A_Q_W, :]) + _dot(ohg, w_ref[A_Q_W:, :])
    x1 = _residual_ln(x_ref[0], y, mod_ref, g_ref, b_ref, 1.0)
    ffn_and_out[-1][0] = _ffn_rows(x1, *ffn_and_out[:-1])


def _hybrid_out_ffn(x, mod, oa, of, ob, hg, nw, w, g, b, ffn, tm):
    bsz, seq, d = x.shape
    tok = pl.BlockSpec((1, tm, d), lambda i, j: (i, j, 0))
    slab = pl.BlockSpec((1, tm, B_W), lambda i, j: (i, j, 0))
    return pl.pallas_call(
        _hyb_out_body,
        grid=(bsz, seq // tm),
        in_specs=[tok, pl.BlockSpec((1, 3, d), lambda i, j: (i, 0, 0)), slab, slab, slab, slab,
                  _resident(nw.shape), _resident(w.shape), _resident(g.shape), _resident(b.shape)]
                 + _ffn_specs(d, ffn),
        out_specs=tok,
        out_shape=jax.ShapeDtypeStruct(x.shape, F32),
        compiler_params=_params("parallel", "parallel"),
        name="hybrid_out_ffn",
    )(x, mod, oa, of, ob, hg, nw, w, g, b, *ffn)


def _rms(x, w_ref):
    return (x * lax.rsqrt(jnp.mean(x * x, -1, keepdims=True) + RMS_EPS) * w_ref[...]).astype(BF16)


def _mla_in_body(x_ref, mod_ref, pos_ref, freq_ref, sgn_ref, wd_ref, qn_ref, kvn_ref, wq_ref, wkv_ref,
                 q_ref, k_ref, v_ref):
    h = _modulate(x_ref[0], mod_ref).astype(BF16)
    down = _dot(h, wd_ref[...])
    ang = pos_ref[0].astype(F32) * freq_ref[...]
    cos = jnp.cos(ang)
    sin = jnp.sin(ang) * sgn_ref[...]
    off = C_Q_RANK + C_KV_RANK
    k_rope = down[:, off:off + LANES] * cos + down[:, off + LANES:off + 2 * LANES] * sin
    cq = _rms(down[:, :C_Q_RANK], qn_ref)
    ckv = _rms(down[:, C_Q_RANK:off], kvn_ref)
    qq = _dot(cq, wq_ref[...])
    kv = _dot(ckv, wkv_ref[...])
    scale = (C_NOPE + C_ROPE) ** -0.5 * LOG2_E
    nh = C_HEADS * LANES
    for hd in range(C_HEADS):
        lanes = slice(hd * LANES, (hd + 1) * LANES)
        q_ref[0, hd] = ((qq[:, lanes] * cos + qq[:, nh + hd * LANES:nh + (hd + 1) * LANES] * sin) * scale).astype(BF16)
        k_ref[0, hd] = (kv[:, lanes] + k_rope).astype(BF16)
    low = lax.broadcasted_iota(jnp.int32, (x_ref.shape[1], LANES), 1) < C_V
    for pr in range(C_HEADS // 2):
        pair = kv[:, nh + pr * LANES:nh + (pr + 1) * LANES]
        v_ref[0, 2 * pr] = jnp.where(low, pair, 1.0).astype(BF16)
        v_ref[0, 2 * pr + 1] = jnp.where(low, 1.0, pair).astype(BF16)


def _mla_in(x, mod, pos, freq, sgn, wd, qn, kvn, wq, wkv, tm):
    bsz, seq, d = x.shape
    tok = pl.BlockSpec((1, tm, d), lambda i, j: (i, j, 0))
    heads = lambda n: pl.BlockSpec((1, n, tm, LANES), lambda i, j: (i, 0, j, 0))
    sds = lambda n: jax.ShapeDtypeStruct((bsz, n, seq, LANES), BF16)
    return pl.pallas_call(
        _mla_in_body,
        grid=(bsz, seq // tm),
        in_specs=[tok, pl.BlockSpec((1, 3, d), lambda i, j: (i, 0, 0)),
                  pl.BlockSpec((1, tm, 1), lambda i, j: (i, j, 0)),
                  _resident(freq.shape), _resident(sgn.shape), _resident(wd.shape),
                  _resident(qn.shape), _resident(kvn.shape), _resident(wq.shape), _resident(wkv.shape)],
        out_specs=[heads(C_HEADS)] * 3,
        out_shape=[sds(C_HEADS)] * 3,
        compiler_params=_params("parallel", "parallel"),
        name="mla_in",
    )(x, mod, pos.reshape(bsz, seq, 1), freq, sgn, wd, qn, kvn, wq, wkv)


def _mla_scores(q, k, s_ref, m_ref):
    s = lax.dot_general(q, k, _NT, preferred_element_type=F32)
    m_ref[...] = jnp.max(s, -1, keepdims=True)
    s_ref[...] = s


def _mla_finish(s_ref, m_ref, v):
    p = jnp.exp2(s_ref[...] - m_ref[...]).astype(BF16)
    o = _dot(p, v)
    return o / pltpu.roll(o, C_V, 1)


def _mla_attn_body(q_ref, k_ref, vp_ref, vc_ref, o_ref, s0_ref, s1_ref, m0_ref, m1_ref, o0_ref):
    @pl.when(pl.program_id(0) == 0)
    def _():
        s1_ref[...] = jnp.zeros_like(s1_ref)
        m1_ref[...] = jnp.zeros_like(m1_ref)
        o0_ref[...] = jnp.zeros_like(o0_ref)

    _mla_scores(q_ref[0, 0], k_ref[0, 0], s0_ref, m0_ref)
    o1 = _mla_finish(s1_ref, m1_ref, vp_ref[0, 0])
    lane = lax.broadcasted_iota(jnp.int32, o1.shape, 1)
    o_ref[0] = jnp.where(lane < C_V, o0_ref[...], o1).astype(BF16)

    _mla_scores(q_ref[0, 1], k_ref[0, 1], s1_ref, m1_ref)
    o0_ref[...] = _mla_finish(s0_ref, m0_ref, vc_ref[0, 0])


def _mla_attention(q, k, v, tq):
    bsz, nh, seq, _ = q.shape
    nq = seq // tq
    n_items = bsz * (nh // 2) * nq

    def item(t):
        t = jnp.minimum(t, n_items - 1)
        return t // (nq * (nh // 2)), (t // nq) % (nh // 2), t % nq

    def prev(t):
        return item(jnp.maximum(t - 1, 0))

    return pl.pallas_call(
        _mla_attn_body,
        grid=(n_items + 1,),
        in_specs=[pl.BlockSpec((1, 2, tq, LANES), lambda t: (item(t)[0], item(t)[1], item(t)[2], 0)),
                  pl.BlockSpec((1, 2, seq, LANES), lambda t: (item(t)[0], item(t)[1], 0, 0)),
                  pl.BlockSpec((1, 1, seq, LANES), lambda t: (prev(t)[0], 2 * prev(t)[1] + 1, 0, 0)),
                  pl.BlockSpec((1, 1, seq, LANES), lambda t: (item(t)[0], 2 * item(t)[1], 0, 0))],
        out_specs=pl.BlockSpec((1, tq, LANES), lambda t: (prev(t)[0], prev(t)[2], prev(t)[1])),
        out_shape=jax.ShapeDtypeStruct((bsz, seq, nh * C_V), BF16),
        scratch_shapes=[pltpu.VMEM((tq, seq), F32), pltpu.VMEM((tq, seq), F32),
                        pltpu.VMEM((tq, 1), F32), pltpu.VMEM((tq, 1), F32), pltpu.VMEM((tq, LANES), F32)],
        compiler_params=_params("arbitrary"),
        name="mla_attention",
    )(q, k, v, v)


def _proj_out_body(x_ref, mod_ref, att_ref, w_ref, g_ref, b_ref, *ffn_and_out):
    y = _dot(att_ref[0], w_ref[...])
    x1 = _residual_ln(x_ref[0], y, mod_ref, g_ref, b_ref, 1.0)
    ffn_and_out[-1][0] = _ffn_rows(x1, *ffn_and_out[:-1])


def _proj_out_ffn(x, mod, att, w, g, b, ffn, tm):
    bsz, seq, d = x.shape
    tok = pl.BlockSpec((1, tm, d), lambda i, j: (i, j, 0))
    return pl.pallas_call(
        _proj_out_body,
        grid=(bsz, seq // tm),
        in_specs=[tok, pl.BlockSpec((1, 3, d), lambda i, j: (i, 0, 0)),
                  pl.BlockSpec((1, tm, att.shape[-1]), lambda i, j: (i, j, 0)),
                  _resident(w.shape), _resident(g.shape), _resident(b.shape)] + _ffn_specs(d, ffn),
        out_specs=tok,
        out_shape=jax.ShapeDtypeStruct(x.shape, F32),
        compiler_params=_params("parallel", "parallel"),
        name="mla_out_ffn",
    )(x, mod, att, w, g, b, *ffn)


def _hybrid_w_in(w):
    d = w.shape[0]
    aq = w[:, :A_Q_W]
    ak = w[:, A_Q_W:A_Q_W + A_KV_W].reshape(d, A_KV_HEADS, 1, A_HEAD_DIM)
    av = w[:, A_Q_W + A_KV_W:A_Q_W + 2 * A_KV_W].reshape(d, A_KV_HEADS, 1, A_HEAD_DIM)
    tile = lambda t: jnp.broadcast_to(t, (d, A_KV_HEADS, A_GROUP, A_HEAD_DIM)).reshape(d, A_Q_W)
    return jnp.concatenate([aq, tile(ak), tile(av), w[:, A_Q_W + 2 * A_KV_W:]], axis=1).astype(BF16)


def _swap_halves(t):
    half = t.shape[-1] // 2
    return jnp.concatenate([t[..., half:], t[..., :half]], axis=-1)


def _mla_weights(w_down, w_uq, w_ukv):
    d = w_down.shape[0]
    off = C_Q_RANK + C_KV_RANK
    pad_rope = lambda t: jnp.pad(t, [(0, 0)] * (t.ndim - 1) + [(C_NOPE, LANES - C_NOPE - C_ROPE)])
    kr = w_down[:, off:]
    wd = jnp.concatenate([w_down[:, :off], pad_rope(kr), pad_rope(_swap_halves(kr))], axis=1)
    uq = w_uq.reshape(C_Q_RANK, C_HEADS, C_NOPE + C_ROPE)
    q_main = jnp.pad(uq, [(0, 0), (0, 0), (0, LANES - C_NOPE - C_ROPE)])
    q_swap = pad_rope(_swap_halves(uq[..., C_NOPE:]))
    wq = jnp.concatenate([q_main.reshape(C_Q_RANK, -1), q_swap.reshape(C_Q_RANK, -1)], axis=1)
    ukv = w_ukv.reshape(C_KV_RANK, C_HEADS, C_NOPE + C_V)
    k_main = jnp.pad(ukv[..., :C_NOPE], [(0, 0), (0, 0), (0, LANES - C_NOPE)])
    wkv = jnp.concatenate([k_main.reshape(C_KV_RANK, -1), ukv[..., C_NOPE:].reshape(C_KV_RANK, -1)], axis=1)
    return wd.astype(BF16), wq.astype(BF16), wkv.astype(BF16)


def _rope_tables():
    half = C_ROPE // 2
    freqs = ROPE_THETA ** (-jnp.arange(half, dtype=F32) / half)
    zeros = jnp.zeros((C_NOPE,), F32)
    tail = jnp.zeros((LANES - C_NOPE - C_ROPE,), F32)
    freq = jnp.concatenate([zeros, freqs, freqs, tail]).reshape(1, LANES)
    ones = jnp.ones((half,), F32)
    sgn = jnp.concatenate([zeros, -ones, ones, tail]).reshape(1, LANES)
    return freq, sgn


def kernel(x, c, positions, ada_w, ada_b, ln_g, ln_b, ffn_w_gate, ffn_w_up, ffn_w_down,
           hyb_w_in, hyb_w_out, attn_sink, hgrn_lb_logits, hgrn_norm_w,
           mla_w_down, mla_q_norm, mla_kv_norm, mla_w_uq, mla_w_ukv, mla_w_out):
    bsz, seq, d = x.shape
    tm = min(512, seq)
    mod_all = _modulation(c, ada_w, ada_b).reshape(DEPTH, bsz, N_SUB, 3, d)
    row = lambda t: t.reshape(1, -1)
    freq, sgn = _rope_tables()
    for layer in range(DEPTH):
        mod = lambda j, l=layer: mod_all[l, :, j]
        ffn = lambda j, s, l=layer: (mod(s), ffn_w_gate[l, j].astype(BF16), ffn_w_up[l, j].astype(BF16),
                                     ffn_w_down[l, j].astype(BF16), row(ln_g[l, s]), row(ln_b[l, s]))
        x = _ffn_sublayer(x, ffn(0, 0), min(1024, seq))
        if layer % 2 == 0:
            e = layer // 2
            aq, ak, av, hq, kf, gf, kb, gb, hv, hg = _hybrid_in(
                x, mod(1), _hybrid_w_in(hyb_w_in[e]), hgrn_lb_logits.astype(F32), layer, tm)
            oa = _window_attention(aq, ak, av, positions, attn_sink[e].astype(F32))
            of, ob = _hgrn2(hq, kf, gf, kb, gb, hv, tm)
            x = _hybrid_out_ffn(x, mod(1), oa, of, ob, hg, row(hgrn_norm_w[e]), hyb_w_out[e].astype(BF16),
                                row(ln_g[layer, 1]), row(ln_b[layer, 1]), ffn(1, 2), tm)
        else:
            o = layer // 2
            wd, wq, wkv = _mla_weights(mla_w_down[o], mla_w_uq[o], mla_w_ukv[o])
            q, k, v = _mla_in(x, mod(1), positions, freq, sgn, wd, row(mla_q_norm[o]), row(mla_kv_norm[o]),
                              wq, wkv, tm)
            att = _mla_attention(q, k, v, tm)
            x = _proj_out_ffn(x, mod(1), att, mla_w_out[o].astype(BF16), row(ln_g[layer, 1]), row(ln_b[layer, 1]),
                              ffn(1, 2), tm)
    return x
```

```python
import functools

import jax
import jax.numpy as jnp
from jax import lax
from jax.experimental import pallas as pl
from jax.experimental.pallas import tpu as pltpu

F32 = jnp.float32
BF16 = jnp.bfloat16

D_MODEL = 1024
DEPTH = 2
D_FF = 2816
A_HEADS = 8
A_KV_HEADS = 2
A_HEAD_DIM = 64
A_GROUP = A_HEADS // A_KV_HEADS
WINDOW = 128
A_BLOCK = 128
B_HEADS = 4
B_KEY_DIM = 128
B_VAL_DIM = 128
B_CHUNK = 64
C_HEADS = 16
C_Q_RANK = 256
C_KV_RANK = 256
C_NOPE = 64
C_ROPE = 32
C_V = 64
ROPE_THETA = 10000.0
LN_EPS = 1e-5
RMS_EPS = 1e-6
LOG2_E = 1.4426950408889634
ALPHA = (2 * DEPTH) ** 0.25
N_SUB = 3
A_Q_W = A_HEADS * A_HEAD_DIM
A_KV_W = A_KV_HEADS * A_HEAD_DIM
B_W = B_HEADS * B_KEY_DIM
LANES = 128
VMEM_LIMIT = 56 * 1024 * 1024

_NT = (((1,), (1,)), ((), ()))
_TN = (((0,), (0,)), ((), ()))


def _params(*sem):
    return pltpu.CompilerParams(dimension_semantics=sem, vmem_limit_bytes=VMEM_LIMIT)


def _resident(shape):
    zeros = (0,) * len(shape)
    return pl.BlockSpec(shape, lambda *_: zeros, pipeline_mode=pl.Buffered(1))


def _dot(a, b):
    return jnp.dot(a, b, preferred_element_type=F32)


def _modulate(x, mod_ref):
    return x * (1.0 + mod_ref[0, 1:2, :]) + mod_ref[0, 0:1, :]


def _residual_ln(x, y, mod_ref, g_ref, b_ref, res_w):
    r = ALPHA * x + res_w * (1.0 + mod_ref[0, 2:3, :]) * y
    mu = jnp.mean(r, -1, keepdims=True)
    d = r - mu
    var = jnp.mean(d * d, -1, keepdims=True)
    return d * lax.rsqrt(var + LN_EPS) * g_ref[...] + b_ref[...]


def _silu(v):
    return v * jax.nn.sigmoid(v)


def _mod_body(c_ref, w_ref, b_ref, o_ref):
    cond = _silu(c_ref[...])
    o_ref[0] = jnp.dot(cond, w_ref[0], preferred_element_type=F32,
                       precision=lax.Precision.HIGHEST) + b_ref[0]


def _modulation(c, ada_w, ada_b):
    bsz, d = c.shape
    n = ada_w.shape[-1]
    tn = 1152
    return pl.pallas_call(
        _mod_body,
        grid=(DEPTH, n // tn),
        in_specs=[pl.BlockSpec((bsz, d), lambda l, j: (0, 0)),
                  pl.BlockSpec((1, d, tn), lambda l, j: (l, 0, j)),
                  pl.BlockSpec((1, 1, tn), lambda l, j: (l, 0, j))],
        out_specs=pl.BlockSpec((1, bsz, tn), lambda l, j: (l, 0, j)),
        out_shape=jax.ShapeDtypeStruct((DEPTH, bsz, n), F32),
        compiler_params=_params("parallel", "parallel"),
        name="adaln_mod",
    )(c, ada_w, ada_b.reshape(DEPTH, 1, n))


def _ffn_rows(x, mod_ref, wg_ref, wu_ref, wd_ref, g_ref, b_ref):
    h = _modulate(x, mod_ref).astype(BF16)
    gate = _dot(h, wg_ref[...])
    up = _dot(h, wu_ref[...])
    act = (_silu(gate) * up).astype(BF16)
    y = _dot(act, wd_ref[...])
    return _residual_ln(x, y, mod_ref, g_ref, b_ref, 0.5)


def _ffn_body(x_ref, mod_ref, *rest):
    o_ref = rest[-1]
    o_ref[0] = _ffn_rows(x_ref[0], mod_ref, *rest[:-1])


def _ffn_specs(d, ffn):
    return [pl.BlockSpec((1, 3, d), lambda i, j: (i, 0, 0))] + [_resident(t.shape) for t in ffn[1:]]


def _ffn_sublayer(x, ffn, tm):
    bsz, seq, d = x.shape
    tok = pl.BlockSpec((1, tm, d), lambda i, j: (i, j, 0))
    return pl.pallas_call(
        _ffn_body,
        grid=(bsz, seq // tm),
        in_specs=[tok] + _ffn_specs(d, ffn),
        out_specs=tok,
        out_shape=jax.ShapeDtypeStruct(x.shape, F32),
        compiler_params=_params("parallel", "parallel"),
        name="ffn_sublayer",
    )(x, *ffn)


def _hyb_in_body(layer, x_ref, mod_ref, w_ref, lbl_ref,
                 aq_ref, ak_ref, av_ref, hq_ref, kf_ref, gf_ref, kb_ref, gb_ref, hv_ref, hg_ref):
    h = _modulate(x_ref[0], mod_ref).astype(BF16)
    col = lambda j: _dot(h, w_ref[:, j * B_W:(j + 1) * B_W])
    aq_ref[0] = (col(0) * (A_HEAD_DIM ** -0.5 * LOG2_E)).astype(BF16)
    ak_ref[0] = col(1).astype(BF16)
    av_ref[0] = col(2).astype(BF16)
    hq_ref[0] = _silu(col(3))
    logits = lbl_ref[...]
    e = jnp.exp(logits - jnp.max(logits, 0, keepdims=True))
    lb = jnp.sum(e[0:layer + 1], 0, keepdims=True) / jnp.sum(e, 0, keepdims=True)
    for src, k_ref, g_ref in ((4, kf_ref, gf_ref), (5, kb_ref, gb_ref)):
        f = lb + (1.0 - lb) * jax.nn.sigmoid(col(src))
        k_ref[0] = 1.0 - f
        g_ref[0] = jnp.log(f)
    hv_ref[0] = col(6)
    hg_ref[0] = _silu(col(7))


def _hybrid_in(x, mod, w, lb_logits, layer, tm):
    bsz, seq, d = x.shape
    tok = pl.BlockSpec((1, tm, d), lambda i, j: (i, j, 0))
    slab = pl.BlockSpec((1, tm, B_W), lambda i, j: (i, j, 0))
    sds = lambda dt: jax.ShapeDtypeStruct((bsz, seq, B_W), dt)
    return pl.pallas_call(
        functools.partial(_hyb_in_body, layer),
        grid=(bsz, seq // tm),
        in_specs=[tok, pl.BlockSpec((1, 3, d), lambda i, j: (i, 0, 0)),
                  _resident(w.shape), _resident(lb_logits.shape)],
        out_specs=[slab] * 10,
        out_shape=[sds(BF16)] * 3 + [sds(F32)] * 7,
        compiler_params=_params("parallel", "parallel"),
        name="hybrid_in",
    )(x, mod, w, lb_logits)


def _win_body(seq, nblk, sink_ref, q_ref, km_ref, k0_ref, kp_ref, vm_ref, v0_ref, vp_ref,
              pc_ref, pm_ref, p0_ref, pp_ref, o_ref):
    blk = A_BLOCK
    gw = A_GROUP * A_HEAD_DIM
    k_all = jnp.concatenate([km_ref[0], k0_ref[0], kp_ref[0]], axis=0)
    v_all = jnp.concatenate([vm_ref[0], v0_ref[0], vp_ref[0]], axis=0)
    pk_all = jnp.concatenate([pm_ref[0], p0_ref[0], pp_ref[0]], axis=1)
    qi = lax.broadcasted_iota(jnp.int32, (blk, 3 * blk), 0) + blk
    ki = lax.broadcasted_iota(jnp.int32, (blk, 3 * blk), 1)
    in_window = jnp.abs(qi - ki) <= WINDOW
    head_of_lane = lax.broadcasted_iota(jnp.int32, (blk, gw), 1) // A_HEAD_DIM
    for r in range(nblk):
        i = pl.program_id(1) * nblk + r
        rows = slice(r * blk, (r + 1) * blk)
        band = slice(r * blk, (r + 3) * blk)
        dist = jnp.abs(pc_ref[0, rows, :] - pk_all[:, band]).astype(F32)
        abs_k = i * blk + ki - blk
        valid = in_window & (abs_k >= 0) & (abs_k < seq)
        for j in range(A_KV_HEADS):
            lanes = slice(j * gw, (j + 1) * gw)
            qg = q_ref[0, rows, lanes]
            qs = jnp.concatenate([jnp.where(head_of_lane == g, qg, jnp.zeros_like(qg))
                                  for g in range(A_GROUP)], axis=0)
            s = lax.dot_general(qs, k_all[band, lanes], _NT, preferred_element_type=F32)
            probs = []
            for g in range(A_GROUP):
                hd = j * A_GROUP + g
                slope = 2.0 ** (-8.0 * (hd + 1) / A_HEADS) * LOG2_E
                sg = s[g * blk:(g + 1) * blk] - slope * dist
                sg = jnp.where(valid, sg, -jnp.inf)
                sk = sink_ref[hd] * LOG2_E
                m = jnp.maximum(jnp.max(sg, -1, keepdims=True), sk)
                p = jnp.exp2(sg - m)
                p = p / (jnp.sum(p, -1, keepdims=True) + jnp.exp2(sk - m))
                probs.append(p.astype(BF16))
            o = _dot(jnp.concatenate(probs, axis=0), v_all[band, lanes])
            acc = jnp.zeros((blk, gw), F32)
            for g in range(A_GROUP):
                acc = acc + jnp.where(head_of_lane == g, o[g * blk:(g + 1) * blk], 0.0)
            o_ref[0, rows, lanes] = acc.astype(BF16)


def _window_attention(aq, ak, av, pos, sink):
    bsz, seq, w = aq.shape
    nb = seq // A_BLOCK
    nblk = min(4, nb)
    cur = lambda i, j: (i, j, 0)
    prv = lambda i, j: (i, jnp.maximum(j * nblk - 1, 0), 0)
    nxt = lambda i, j: (i, jnp.minimum((j + 1) * nblk, nb - 1), 0)
    edge = lambda f: pl.BlockSpec((1, A_BLOCK, w), f)
    mid = pl.BlockSpec((1, nblk * A_BLOCK, w), cur)
    row = lambda n, f: pl.BlockSpec((1, 1, n), lambda i, j: (f(i, j)[0], 0, f(i, j)[1]))
    pos_row = pos.reshape(bsz, 1, seq)
    return pl.pallas_call(
        functools.partial(_win_body, seq, nblk),
        grid=(bsz, nb // nblk),
        in_specs=[pl.BlockSpec(memory_space=pltpu.SMEM),
                  mid, edge(prv), mid, edge(nxt), edge(prv), mid, edge(nxt),
                  pl.BlockSpec((1, nblk * A_BLOCK, 1), cur),
                  row(A_BLOCK, prv), row(nblk * A_BLOCK, cur), row(A_BLOCK, nxt)],
        out_specs=mid,
        out_shape=jax.ShapeDtypeStruct((bsz, seq, w), BF16),
        compiler_params=_params("parallel", "parallel"),
        name="window_attention",
    )(sink, aq, ak, ak, ak, av, av, av, pos.reshape(bsz, seq, 1), pos_row, pos_row, pos_row)


def _cumsum_rows(tri, logf):
    hi = logf.astype(BF16)
    lo = (logf - hi.astype(F32)).astype(BF16)
    return _dot(tri, hi) + _dot(tri, lo)


def _hgrn_chunk(rows, tri_cum, tri_keep, mid, end, q_ref, k_ref, g_ref, v_ref, st_ref, o_ref):
    b_all = _cumsum_rows(tri_cum, g_ref[0, rows, :])
    for h in range(B_HEADS):
        lanes = slice(h * B_KEY_DIM, (h + 1) * B_KEY_DIM)
        b = b_all[:, lanes]
        b_mid = b[mid:mid + 1]
        b_end = b[end:end + 1]
        q = q_ref[0, rows, lanes]
        k = k_ref[0, rows, lanes]
        v = v_ref[0, rows, lanes].astype(BF16)
        a = lax.dot_general((q * jnp.exp(b - b_mid)).astype(BF16), (k * jnp.exp(b_mid - b)).astype(BF16),
                            _NT, preferred_element_type=F32)
        o = _dot(jnp.where(tri_keep, a, 0.0).astype(BF16), v)
        st = st_ref[h]
        o = o + lax.dot_general((q * jnp.exp(b)).astype(BF16), st.astype(BF16), _NT,
                                preferred_element_type=F32)
        o_ref[0, rows, lanes] = o
        d_st = lax.dot_general(v, (k * jnp.exp(b_end - b)).astype(BF16), _TN, preferred_element_type=F32)
        st_ref[h] = st * jnp.exp(b_end) + d_st


def _hgrn_body(n_chunks, qf_ref, kf_ref, gf_ref, vf_ref, qb_ref, kb_ref, gb_ref, vb_ref,
               of_ref, ob_ref, sf_ref, sb_ref):
    @pl.when(pl.program_id(1) == 0)
    def _():
        sf_ref[...] = jnp.zeros_like(sf_ref)
        sb_ref[...] = jnp.zeros_like(sb_ref)

    r = lax.broadcasted_iota(jnp.int32, (B_CHUNK, B_CHUNK), 0)
    c = lax.broadcasted_iota(jnp.int32, (B_CHUNK, B_CHUNK), 1)
    lower = r >= c
    upper = r <= c
    lower_ones = lower.astype(BF16)
    upper_ones = upper.astype(BF16)

    for t in range(n_chunks):
        rows_f = slice(t * B_CHUNK, (t + 1) * B_CHUNK)
        _hgrn_chunk(rows_f, lower_ones, lower, B_CHUNK // 2 - 1, B_CHUNK - 1,
                    qf_ref, kf_ref, gf_ref, vf_ref, sf_ref, of_ref)
        rows_b = slice((n_chunks - 1 - t) * B_CHUNK, (n_chunks - t) * B_CHUNK)
        _hgrn_chunk(rows_b, upper_ones, upper, B_CHUNK // 2, 0,
                    qb_ref, kb_ref, gb_ref, vb_ref, sb_ref, ob_ref)


def _hgrn2(hq, kf, gf, kb, gb, hv, tb):
    bsz, seq, w = hq.shape
    n = seq // tb
    fwd = pl.BlockSpec((1, tb, w), lambda i, j: (i, j, 0))
    bwd = pl.BlockSpec((1, tb, w), lambda i, j: (i, n - 1 - j, 0))
    out = jax.ShapeDtypeStruct((bsz, seq, w), F32)
    state = pltpu.VMEM((B_HEADS, B_VAL_DIM, B_KEY_DIM), F32)
    return pl.pallas_call(
        functools.partial(_hgrn_body, tb // B_CHUNK),
        grid=(bsz, n),
        in_specs=[fwd] * 4 + [bwd] * 4,
        out_specs=[fwd, bwd],
        out_shape=[out, out],
        scratch_shapes=[state, state],
        compiler_params=_params("parallel", "arbitrary"),
        name="hgrn2_scan",
    )(hq, kf, gf, hv, hq, kb, gb, hv)


def _hyb_out_body(x_ref, mod_ref, oa_ref, of_ref, ob_ref, hg_ref, nw_ref, w_ref, g_ref, b_ref, *ffn_and_out):
    o = of_ref[0] + ob_ref[0]
    heads = []
    for h in range(B_HEADS):
        oh = o[:, h * B_VAL_DIM:(h + 1) * B_VAL_DIM]
        heads.append(oh * lax.rsqrt(jnp.mean(oh * oh, -1, keepdims=True) + RMS_EPS))
    ohg = (jnp.concatenate(heads, axis=1) * nw_ref[...] * hg_ref[0]).astype(BF16)
    y = _dot(oa_ref[0], w_ref[0:A_Q_W, :]) + _dot(ohg, w_ref[A_Q_W:, :])
    x1 = _residual_ln(x_ref[0], y, mod_ref, g_ref, b_ref, 1.0)
    ffn_and_out[-1][0] = _ffn_rows(x1, *ffn_and_out[:-1])


def _hybrid_out_ffn(x, mod, oa, of, ob, hg, nw, w, g, b, ffn, tm):
    bsz, seq, d = x.shape
    tok = pl.BlockSpec((1, tm, d), lambda i, j: (i, j, 0))
    slab = pl.BlockSpec((1, tm, B_W), lambda i, j: (i, j, 0))
    return pl.pallas_call(
        _hyb_out_body,
        grid=(bsz, seq // tm),
        in_specs=[tok, pl.BlockSpec((1, 3, d), lambda i, j: (i, 0, 0)), slab, slab, slab, slab,
                  _resident(nw.shape), _resident(w.shape), _resident(g.shape), _resident(b.shape)]
                 + _ffn_specs(d, ffn),
        out_specs=tok,
        out_shape=jax.ShapeDtypeStruct(x.shape, F32),
        compiler_params=_params("parallel", "parallel"),
        name="hybrid_out_ffn",
    )(x, mod, oa, of, ob, hg, nw, w, g, b, *ffn)


def _rms(x, w_ref):
    return (x * lax.rsqrt(jnp.mean(x * x, -1, keepdims=True) + RMS_EPS) * w_ref[...]).astype(BF16)


def _mla_in_body(x_ref, mod_ref, pos_ref, freq_ref, sgn_ref, wd_ref, qn_ref, kvn_ref, wq_ref, wkv_ref,
                 q_ref, k_ref, v_ref):
    h = _modulate(x_ref[0], mod_ref).astype(BF16)
    down = _dot(h, wd_ref[...])
    ang = pos_ref[0].astype(F32) * freq_ref[...]
    cos = jnp.cos(ang)
    sin = jnp.sin(ang) * sgn_ref[...]
    off = C_Q_RANK + C_KV_RANK
    k_rope = down[:, off:off + LANES] * cos + down[:, off + LANES:off + 2 * LANES] * sin
    cq = _rms(down[:, :C_Q_RANK], qn_ref)
    ckv = _rms(down[:, C_Q_RANK:off], kvn_ref)
    qq = _dot(cq, wq_ref[...])
    kv = _dot(ckv, wkv_ref[...])
    scale = (C_NOPE + C_ROPE) ** -0.5 * LOG2_E
    nh = C_HEADS * LANES
    for hd in range(C_HEADS):
        lanes = slice(hd * LANES, (hd + 1) * LANES)
        q_ref[0, hd] = ((qq[:, lanes] * cos + qq[:, nh + hd * LANES:nh + (hd + 1) * LANES] * sin) * scale).astype(BF16)
        k_ref[0, hd] = (kv[:, lanes] + k_rope).astype(BF16)
    low = lax.broadcasted_iota(jnp.int32, (x_ref.shape[1], LANES), 1) < C_V
    for pr in range(C_HEADS // 2):
        pair = kv[:, nh + pr * LANES:nh + (pr + 1) * LANES]
        v_ref[0, 2 * pr] = jnp.where(low, pair, 1.0).astype(BF16)
        v_ref[0, 2 * pr + 1] = jnp.where(low, 1.0, pair).astype(BF16)


def _mla_in(x, mod, pos, freq, sgn, wd, qn, kvn, wq, wkv, tm):
    bsz, seq, d = x.shape
    tok = pl.BlockSpec((1, tm, d), lambda i, j: (i, j, 0))
    heads = lambda n: pl.BlockSpec((1, n, tm, LANES), lambda i, j: (i, 0, j, 0))
    sds = lambda n: jax.ShapeDtypeStruct((bsz, n, seq, LANES), BF16)
    return pl.pallas_call(
        _mla_in_body,
        grid=(bsz, seq // tm),
        in_specs=[tok, pl.BlockSpec((1, 3, d), lambda i, j: (i, 0, 0)),
                  pl.BlockSpec((1, tm, 1), lambda i, j: (i, j, 0)),
                  _resident(freq.shape), _resident(sgn.shape), _resident(wd.shape),
                  _resident(qn.shape), _resident(kvn.shape), _resident(wq.shape), _resident(wkv.shape)],
        out_specs=[heads(C_HEADS)] * 3,
        out_shape=[sds(C_HEADS)] * 3,
        compiler_params=_params("parallel", "parallel"),
        name="mla_in",
    )(x, mod, pos.reshape(bsz, seq, 1), freq, sgn, wd, qn, kvn, wq, wkv)


def _mla_scores(q, k, s_ref, m_ref):
    s = lax.dot_general(q, k, _NT, preferred_element_type=F32)
    m_ref[...] = jnp.max(s, -1, keepdims=True)
    s_ref[...] = s


def _mla_finish(s_ref, m_ref, v):
    p = jnp.exp2(s_ref[...] - m_ref[...]).astype(BF16)
    o = _dot(p, v)
    return o / pltpu.roll(o, C_V, 1)


def _mla_attn_body(q_ref, k_ref, vpo_ref, ve_ref, vo_ref, oa_ref, ob_ref,
                   s0_ref, s1_ref, m0_ref, m1_ref, o0b_ref):
    @pl.when(pl.program_id(0) == 0)
    def _():
        s1_ref[...] = jnp.zeros_like(s1_ref)
        m1_ref[...] = jnp.zeros_like(m1_ref)
        o0b_ref[...] = jnp.zeros_like(o0b_ref)

    tq = s0_ref.shape[0]
    qa, qb = slice(0, tq), slice(tq, 2 * tq)
    low = lax.broadcasted_iota(jnp.int32, (tq, LANES), 1) < C_V

    _mla_scores(q_ref[0, 0, qa, :], k_ref[0, 0], s0_ref, m0_ref)
    ob_ref[0] = jnp.where(low, o0b_ref[...], _mla_finish(s1_ref, m1_ref, vpo_ref[0, 0])).astype(BF16)

    _mla_scores(q_ref[0, 1, qa, :], k_ref[0, 1], s1_ref, m1_ref)
    o0a = _mla_finish(s0_ref, m0_ref, ve_ref[0, 0])

    _mla_scores(q_ref[0, 0, qb, :], k_ref[0, 0], s0_ref, m0_ref)
    oa_ref[0] = jnp.where(low, o0a, _mla_finish(s1_ref, m1_ref, vo_ref[0, 0])).astype(BF16)

    _mla_scores(q_ref[0, 1, qb, :], k_ref[0, 1], s1_ref, m1_ref)
    o0b_ref[...] = _mla_finish(s0_ref, m0_ref, ve_ref[0, 0])


def _mla_attention(q, k, v, tq):
    bsz, nh, seq, _ = q.shape
    ni = seq // (2 * tq)
    n_items = bsz * (nh // 2) * ni

    def item(t):
        t = jnp.minimum(t, n_items - 1)
        return t // (ni * (nh // 2)), (t // ni) % (nh // 2), t % ni

    def prev(t):
        return item(jnp.maximum(t - 1, 0))

    head = lambda f, odd: pl.BlockSpec((1, 1, seq, LANES), lambda t: (f(t)[0], 2 * f(t)[1] + odd, 0, 0))
    tile = lambda f: pl.BlockSpec((1, tq, LANES), lambda t: (f(t)[0], f(t)[2], f(t)[1]))
    out = jax.ShapeDtypeStruct((bsz, seq // 2, nh * C_V), BF16)
    return pl.pallas_call(
        _mla_attn_body,
        grid=(n_items + 1,),
        in_specs=[pl.BlockSpec((1, 2, 2 * tq, LANES), lambda t: (item(t)[0], item(t)[1], item(t)[2], 0)),
                  pl.BlockSpec((1, 2, seq, LANES), lambda t: (item(t)[0], item(t)[1], 0, 0)),
                  head(prev, 1), head(item, 0), head(item, 1)],
        out_specs=[tile(item), tile(prev)],
        out_shape=[out, out],
        scratch_shapes=[pltpu.VMEM((tq, seq), F32), pltpu.VMEM((tq, seq), F32),
                        pltpu.VMEM((tq, 1), F32), pltpu.VMEM((tq, 1), F32), pltpu.VMEM((tq, LANES), F32)],
        compiler_params=_params("arbitrary"),
        name="mla_attention",
    )(q, k, v, v, v)


def _proj_out_body(x_ref, mod_ref, atta_ref, attb_ref, w_ref, g_ref, b_ref, *ffn_and_out):
    att = jnp.where(pl.program_id(1) % 2 == 0, atta_ref[0], attb_ref[0])
    y = _dot(att, w_ref[...])
    x1 = _residual_ln(x_ref[0], y, mod_ref, g_ref, b_ref, 1.0)
    ffn_and_out[-1][0] = _ffn_rows(x1, *ffn_and_out[:-1])


def _proj_out_ffn(x, mod, att_a, att_b, w, g, b, ffn, tm):
    bsz, seq, d = x.shape
    tok = pl.BlockSpec((1, tm, d), lambda i, j: (i, j, 0))
    half = pl.BlockSpec((1, tm, att_a.shape[-1]), lambda i, j: (i, j // 2, 0))
    return pl.pallas_call(
        _proj_out_body,
        grid=(bsz, seq // tm),
        in_specs=[tok, pl.BlockSpec((1, 3, d), lambda i, j: (i, 0, 0)), half, half,
                  _resident(w.shape), _resident(g.shape), _resident(b.shape)] + _ffn_specs(d, ffn),
        out_specs=tok,
        out_shape=jax.ShapeDtypeStruct(x.shape, F32),
        compiler_params=_params("parallel", "parallel"),
        name="mla_out_ffn",
    )(x, mod, att_a, att_b, w, g, b, *ffn)


def _hybrid_w_in(w):
    d = w.shape[0]
    aq = w[:, :A_Q_W]
    ak = w[:, A_Q_W:A_Q_W + A_KV_W].reshape(d, A_KV_HEADS, 1, A_HEAD_DIM)
    av = w[:, A_Q_W + A_KV_W:A_Q_W + 2 * A_KV_W].reshape(d, A_KV_HEADS, 1, A_HEAD_DIM)
    tile = lambda t: jnp.broadcast_to(t, (d, A_KV_HEADS, A_GROUP, A_HEAD_DIM)).reshape(d, A_Q_W)
    return jnp.concatenate([aq, tile(ak), tile(av), w[:, A_Q_W + 2 * A_KV_W:]], axis=1).astype(BF16)


def _swap_halves(t):
    half = t.shape[-1] // 2
    return jnp.concatenate([t[..., half:], t[..., :half]], axis=-1)


def _mla_weights(w_down, w_uq, w_ukv):
    d = w_down.shape[0]
    off = C_Q_RANK + C_KV_RANK
    pad_rope = lambda t: jnp.pad(t, [(0, 0)] * (t.ndim - 1) + [(C_NOPE, LANES - C_NOPE - C_ROPE)])
    kr = w_down[:, off:]
    wd = jnp.concatenate([w_down[:, :off], pad_rope(kr), pad_rope(_swap_halves(kr))], axis=1)
    uq = w_uq.reshape(C_Q_RANK, C_HEADS, C_NOPE + C_ROPE)
    q_main = jnp.pad(uq, [(0, 0), (0, 0), (0, LANES - C_NOPE - C_ROPE)])
    q_swap = pad_rope(_swap_halves(uq[..., C_NOPE:]))
    wq = jnp.concatenate([q_main.reshape(C_Q_RANK, -1), q_swap.reshape(C_Q_RANK, -1)], axis=1)
    ukv = w_ukv.reshape(C_KV_RANK, C_HEADS, C_NOPE + C_V)
    k_main = jnp.pad(ukv[..., :C_NOPE], [(0, 0), (0, 0), (0, LANES - C_NOPE)])
    wkv = jnp.concatenate([k_main.reshape(C_KV_RANK, -1), ukv[..., C_NOPE:].reshape(C_KV_RANK, -1)], axis=1)
    return wd.astype(BF16), wq.astype(BF16), wkv.astype(BF16)


def _rope_tables():
    half = C_ROPE // 2
    freqs = ROPE_THETA ** (-jnp.arange(half, dtype=F32) / half)
    zeros = jnp.zeros((C_NOPE,), F32)
    tail = jnp.zeros((LANES - C_NOPE - C_ROPE,), F32)
    freq = jnp.concatenate([zeros, freqs, freqs, tail]).reshape(1, LANES)
    ones = jnp.ones((half,), F32)
    sgn = jnp.concatenate([zeros, -ones, ones, tail]).reshape(1, LANES)
    return freq, sgn


def kernel(x, c, positions, ada_w, ada_b, ln_g, ln_b, ffn_w_gate, ffn_w_up, ffn_w_down,
           hyb_w_in, hyb_w_out, attn_sink, hgrn_lb_logits, hgrn_norm_w,
           mla_w_down, mla_q_norm, mla_kv_norm, mla_w_uq, mla_w_ukv, mla_w_out):
    bsz, seq, d = x.shape
    tm = min(512, seq)
    mod_all = _modulation(c, ada_w, ada_b).reshape(DEPTH, bsz, N_SUB, 3, d)
    row = lambda t: t.reshape(1, -1)
    freq, sgn = _rope_tables()
    for layer in range(DEPTH):
        mod = lambda j, l=layer: mod_all[l, :, j]
        ffn = lambda j, s, l=layer: (mod(s), ffn_w_gate[l, j].astype(BF16), ffn_w_up[l, j].astype(BF16),
                                     ffn_w_down[l, j].astype(BF16), row(ln_g[l, s]), row(ln_b[l, s]))
        x = _ffn_sublayer(x, ffn(0, 0), min(1024, seq))
        if layer % 2 == 0:
            e = layer // 2
            aq, ak, av, hq, kf, gf, kb, gb, hv, hg = _hybrid_in(
                x, mod(1), _hybrid_w_in(hyb_w_in[e]), hgrn_lb_logits.astype(F32), layer, tm)
            oa = _window_attention(aq, ak, av, positions, attn_sink[e].astype(F32))
            of, ob = _hgrn2(hq, kf, gf, kb, gb, hv, tm)
            x = _hybrid_out_ffn(x, mod(1), oa, of, ob, hg, row(hgrn_norm_w[e]), hyb_w_out[e].astype(BF16),
                                row(ln_g[layer, 1]), row(ln_b[layer, 1]), ffn(1, 2), tm)
        else:
            o = layer // 2
            wd, wq, wkv = _mla_weights(mla_w_down[o], mla_w_uq[o], mla_w_ukv[o])
            q, k, v = _mla_in(x, mod(1), positions, freq, sgn, wd, row(mla_q_norm[o]), row(mla_kv_norm[o]),
                              wq, wkv, tm)
            tq = min(512, seq // 2)
            att_a, att_b = _mla_attention(q, k, v, tq)
            x = _proj_out_ffn(x, mod(1), att_a, att_b, mla_w_out[o].astype(BF16),
                              row(ln_g[layer, 1]), row(ln_b[layer, 1]), ffn(1, 2), tq)
    return x
```

```python
import functools

import jax
import jax.numpy as jnp
from jax import lax
from jax.experimental import pallas as pl
from jax.experimental.pallas import tpu as pltpu

F32 = jnp.float32
BF16 = jnp.bfloat16

D_MODEL = 1024
DEPTH = 2
D_FF = 2816
A_HEADS = 8
A_KV_HEADS = 2
A_HEAD_DIM = 64
A_GROUP = A_HEADS // A_KV_HEADS
WINDOW = 128
A_BLOCK = 128
B_HEADS = 4
B_KEY_DIM = 128
B_VAL_DIM = 128
B_CHUNK = 64
C_HEADS = 16
C_Q_RANK = 256
C_KV_RANK = 256
C_NOPE = 64
C_ROPE = 32
C_V = 64
ROPE_THETA = 10000.0
LN_EPS = 1e-5
RMS_EPS = 1e-6
LOG2_E = 1.4426950408889634
ALPHA = (2 * DEPTH) ** 0.25
N_SUB = 3
A_Q_W = A_HEADS * A_HEAD_DIM
A_KV_W = A_KV_HEADS * A_HEAD_DIM
B_W = B_HEADS * B_KEY_DIM
LANES = 128
VMEM_LIMIT = 56 * 1024 * 1024

_NT = (((1,), (1,)), ((), ()))
_TN = (((0,), (0,)), ((), ()))


def _params(*sem):
    return pltpu.CompilerParams(dimension_semantics=sem, vmem_limit_bytes=VMEM_LIMIT)


def _resident(shape):
    zeros = (0,) * len(shape)
    return pl.BlockSpec(shape, lambda *_: zeros, pipeline_mode=pl.Buffered(1))


def _dot(a, b):
    return jnp.dot(a, b, preferred_element_type=F32)


def _modulate(x, mod_ref):
    return x * (1.0 + mod_ref[0, 1:2, :]) + mod_ref[0, 0:1, :]


def _residual_ln(x, y, mod_ref, g_ref, b_ref, res_w):
    r = ALPHA * x + res_w * (1.0 + mod_ref[0, 2:3, :]) * y
    mu = jnp.mean(r, -1, keepdims=True)
    d = r - mu
    var = jnp.mean(d * d, -1, keepdims=True)
    return d * lax.rsqrt(var + LN_EPS) * g_ref[...] + b_ref[...]


def _silu(v):
    return v * jax.nn.sigmoid(v)


def _mod_body(c_ref, w_ref, b_ref, o_ref):
    cond = _silu(c_ref[...])
    o_ref[0] = jnp.dot(cond, w_ref[0], preferred_element_type=F32,
                       precision=lax.Precision.HIGHEST) + b_ref[0]


def _modulation(c, ada_w, ada_b):
    bsz, d = c.shape
    n = ada_w.shape[-1]
    tn = 1152
    return pl.pallas_call(
        _mod_body,
        grid=(DEPTH, n // tn),
        in_specs=[pl.BlockSpec((bsz, d), lambda l, j: (0, 0)),
                  pl.BlockSpec((1, d, tn), lambda l, j: (l, 0, j)),
                  pl.BlockSpec((1, 1, tn), lambda l, j: (l, 0, j))],
        out_specs=pl.BlockSpec((1, bsz, tn), lambda l, j: (l, 0, j)),
        out_shape=jax.ShapeDtypeStruct((DEPTH, bsz, n), F32),
        compiler_params=_params("parallel", "parallel"),
        name="adaln_mod",
    )(c, ada_w, ada_b.reshape(DEPTH, 1, n))


def _ffn_rows(x, mod_ref, wg_ref, wu_ref, wd_ref, g_ref, b_ref):
    h = _modulate(x, mod_ref).astype(BF16)
    gate = _dot(h, wg_ref[...])
    up = _dot(h, wu_ref[...])
    act = (_silu(gate) * up).astype(BF16)
    y = _dot(act, wd_ref[...])
    return _residual_ln(x, y, mod_ref, g_ref, b_ref, 0.5)


def _ffn_body(x_ref, mod_ref, *rest):
    o_ref = rest[-1]
    o_ref[0] = _ffn_rows(x_ref[0], mod_ref, *rest[:-1])


def _ffn_specs(d, ffn):
    return [pl.BlockSpec((1, 3, d), lambda i, j: (i, 0, 0))] + [_resident(t.shape) for t in ffn[1:]]


def _ffn_sublayer(x, ffn, tm):
    bsz, seq, d = x.shape
    tok = pl.BlockSpec((1, tm, d), lambda i, j: (i, j, 0))
    return pl.pallas_call(
        _ffn_body,
        grid=(bsz, seq // tm),
        in_specs=[tok] + _ffn_specs(d, ffn),
        out_specs=tok,
        out_shape=jax.ShapeDtypeStruct(x.shape, F32),
        compiler_params=_params("parallel", "parallel"),
        name="ffn_sublayer",
    )(x, *ffn)


def _hyb_in_body(layer, x_ref, mod_ref, w_ref, lbl_ref,
                 aq_ref, ak_ref, av_ref, hq_ref, kf_ref, gf_ref, kb_ref, gb_ref, hv_ref, hg_ref):
    h = _modulate(x_ref[0], mod_ref).astype(BF16)
    col = lambda j: _dot(h, w_ref[:, j * B_W:(j + 1) * B_W])
    aq_ref[0] = (col(0) * (A_HEAD_DIM ** -0.5 * LOG2_E)).astype(BF16)
    ak_ref[0] = col(1).astype(BF16)
    av_ref[0] = col(2).astype(BF16)
    hq_ref[0] = _silu(col(3))
    logits = lbl_ref[...]
    e = jnp.exp(logits - jnp.max(logits, 0, keepdims=True))
    lb = jnp.sum(e[0:layer + 1], 0, keepdims=True) / jnp.sum(e, 0, keepdims=True)
    for src, k_ref, g_ref in ((4, kf_ref, gf_ref), (5, kb_ref, gb_ref)):
        f = lb + (1.0 - lb) * jax.nn.sigmoid(col(src))
        k_ref[0] = 1.0 - f
        g_ref[0] = jnp.log(f)
    hv_ref[0] = col(6)
    hg_ref[0] = _silu(col(7))


def _hybrid_in(x, mod, w, lb_logits, layer, tm):
    bsz, seq, d = x.shape
    tok = pl.BlockSpec((1, tm, d), lambda i, j: (i, j, 0))
    slab = pl.BlockSpec((1, tm, B_W), lambda i, j: (i, j, 0))
    sds = lambda dt: jax.ShapeDtypeStruct((bsz, seq, B_W), dt)
    return pl.pallas_call(
        functools.partial(_hyb_in_body, layer),
        grid=(bsz, seq // tm),
        in_specs=[tok, pl.BlockSpec((1, 3, d), lambda i, j: (i, 0, 0)),
                  _resident(w.shape), _resident(lb_logits.shape)],
        out_specs=[slab] * 10,
        out_shape=[sds(BF16)] * 3 + [sds(F32)] * 7,
        compiler_params=_params("parallel", "parallel"),
        name="hybrid_in",
    )(x, mod, w, lb_logits)


def _win_body(seq, nblk, sink_ref, q_ref, km_ref, k0_ref, kp_ref, vm_ref, v0_ref, vp_ref,
              pc_ref, pm_ref, p0_ref, pp_ref, o_ref):
    blk = A_BLOCK
    gw = A_GROUP * A_HEAD_DIM
    k_all = jnp.concatenate([km_ref[0], k0_ref[0], kp_ref[0]], axis=0)
    v_all = jnp.concatenate([vm_ref[0], v0_ref[0], vp_ref[0]], axis=0)
    pk_all = jnp.concatenate([pm_ref[0], p0_ref[0], pp_ref[0]], axis=1)
    qi = lax.broadcasted_iota(jnp.int32, (blk, 3 * blk), 0) + blk
    ki = lax.broadcasted_iota(jnp.int32, (blk, 3 * blk), 1)
    in_window = jnp.abs(qi - ki) <= WINDOW
    head_of_lane = lax.broadcasted_iota(jnp.int32, (blk, gw), 1) // A_HEAD_DIM
    for r in range(nblk):
        i = pl.program_id(1) * nblk + r
        rows = slice(r * blk, (r + 1) * blk)
        band = slice(r * blk, (r + 3) * blk)
        dist = jnp.abs(pc_ref[0, rows, :] - pk_all[:, band]).astype(F32)
        abs_k = i * blk + ki - blk
        valid = in_window & (abs_k >= 0) & (abs_k < seq)
        for j in range(A_KV_HEADS):
            lanes = slice(j * gw, (j + 1) * gw)
            qg = q_ref[0, rows, lanes]
            qs = jnp.concatenate([jnp.where(head_of_lane == g, qg, jnp.zeros_like(qg))
                                  for g in range(A_GROUP)], axis=0)
            s = lax.dot_general(qs, k_all[band, lanes], _NT, preferred_element_type=F32)
            probs = []
            for g in range(A_GROUP):
                hd = j * A_GROUP + g
                slope = 2.0 ** (-8.0 * (hd + 1) / A_HEADS) * LOG2_E
                sg = s[g * blk:(g + 1) * blk] - slope * dist
                sg = jnp.where(valid, sg, -jnp.inf)
                sk = sink_ref[hd] * LOG2_E
                m = jnp.maximum(jnp.max(sg, -1, keepdims=True), sk)
                p = jnp.exp2(sg - m)
                p = p / (jnp.sum(p, -1, keepdims=True) + jnp.exp2(sk - m))
                probs.append(p.astype(BF16))
            o = _dot(jnp.concatenate(probs, axis=0), v_all[band, lanes])
            acc = jnp.zeros((blk, gw), F32)
            for g in range(A_GROUP):
                acc = acc + jnp.where(head_of_lane == g, o[g * blk:(g + 1) * blk], 0.0)
            o_ref[0, rows, lanes] = acc.astype(BF16)


def _window_attention(aq, ak, av, pos, sink):
    bsz, seq, w = aq.shape
    nb = seq // A_BLOCK
    nblk = min(4, nb)
    cur = lambda i, j: (i, j, 0)
    prv = lambda i, j: (i, jnp.maximum(j * nblk - 1, 0), 0)
    nxt = lambda i, j: (i, jnp.minimum((j + 1) * nblk, nb - 1), 0)
    edge = lambda f: pl.BlockSpec((1, A_BLOCK, w), f)
    mid = pl.BlockSpec((1, nblk * A_BLOCK, w), cur)
    row = lambda n, f: pl.BlockSpec((1, 1, n), lambda i, j: (f(i, j)[0], 0, f(i, j)[1]))
    pos_row = pos.reshape(bsz, 1, seq)
    return pl.pallas_call(
        functools.partial(_win_body, seq, nblk),
        grid=(bsz, nb // nblk),
        in_specs=[pl.BlockSpec(memory_space=pltpu.SMEM),
                  mid, edge(prv), mid, edge(nxt), edge(prv), mid, edge(nxt),
                  pl.BlockSpec((1, nblk * A_BLOCK, 1), cur),
                  row(A_BLOCK, prv), row(nblk * A_BLOCK, cur), row(A_BLOCK, nxt)],
        out_specs=mid,
        out_shape=jax.ShapeDtypeStruct((bsz, seq, w), BF16),
        compiler_params=_params("parallel", "parallel"),
        name="window_attention",
    )(sink, aq, ak, ak, ak, av, av, av, pos.reshape(bsz, seq, 1), pos_row, pos_row, pos_row)


def _cumsum_rows(tri, logf):
    hi = logf.astype(BF16)
    lo = (logf - hi.astype(F32)).astype(BF16)
    return _dot(tri, hi) + _dot(tri, lo)


def _hgrn_chunk(rows, tri_cum, tri_keep, mid, end, q_ref, k_ref, g_ref, v_ref, st_ref, o_ref):
    b_all = _cumsum_rows(tri_cum, g_ref[0, rows, :])
    for h in range(B_HEADS):
        lanes = slice(h * B_KEY_DIM, (h + 1) * B_KEY_DIM)
        b = b_all[:, lanes]
        b_mid = b[mid:mid + 1]
        b_end = b[end:end + 1]
        q = q_ref[0, rows, lanes]
        k = k_ref[0, rows, lanes]
        v = v_ref[0, rows, lanes].astype(BF16)
        a = lax.dot_general((q * jnp.exp(b - b_mid)).astype(BF16), (k * jnp.exp(b_mid - b)).astype(BF16),
                            _NT, preferred_element_type=F32)
        o = _dot(jnp.where(tri_keep, a, 0.0).astype(BF16), v)
        st = st_ref[h]
        o = o + lax.dot_general((q * jnp.exp(b)).astype(BF16), st.astype(BF16), _NT,
                                preferred_element_type=F32)
        o_ref[0, rows, lanes] = o
        d_st = lax.dot_general(v, (k * jnp.exp(b_end - b)).astype(BF16), _TN, preferred_element_type=F32)
        st_ref[h] = st * jnp.exp(b_end) + d_st


def _hgrn_body(n_chunks, qf_ref, kf_ref, gf_ref, vf_ref, qb_ref, kb_ref, gb_ref, vb_ref,
               of_ref, ob_ref, sf_ref, sb_ref):
    @pl.when(pl.program_id(1) == 0)
    def _():
        sf_ref[...] = jnp.zeros_like(sf_ref)
        sb_ref[...] = jnp.zeros_like(sb_ref)

    r = lax.broadcasted_iota(jnp.int32, (B_CHUNK, B_CHUNK), 0)
    c = lax.broadcasted_iota(jnp.int32, (B_CHUNK, B_CHUNK), 1)
    lower = r >= c
    upper = r <= c
    lower_ones = lower.astype(BF16)
    upper_ones = upper.astype(BF16)

    for t in range(n_chunks):
        rows_f = slice(t * B_CHUNK, (t + 1) * B_CHUNK)
        _hgrn_chunk(rows_f, lower_ones, lower, B_CHUNK // 2 - 1, B_CHUNK - 1,
                    qf_ref, kf_ref, gf_ref, vf_ref, sf_ref, of_ref)
        rows_b = slice((n_chunks - 1 - t) * B_CHUNK, (n_chunks - t) * B_CHUNK)
        _hgrn_chunk(rows_b, upper_ones, upper, B_CHUNK // 2, 0,
                    qb_ref, kb_ref, gb_ref, vb_ref, sb_ref, ob_ref)


def _hgrn2(hq, kf, gf, kb, gb, hv, tb):
    bsz, seq, w = hq.shape
    n = seq // tb
    fwd = pl.BlockSpec((1, tb, w), lambda i, j: (i, j, 0))
    bwd = pl.BlockSpec((1, tb, w), lambda i, j: (i, n - 1 - j, 0))
    out = jax.ShapeDtypeStruct((bsz, seq, w), F32)
    state = pltpu.VMEM((B_HEADS, B_VAL_DIM, B_KEY_DIM), F32)
    return pl.pallas_call(
        functools.partial(_hgrn_body, tb // B_CHUNK),
        grid=(bsz, n),
        in_specs=[fwd] * 4 + [bwd] * 4,
        out_specs=[fwd, bwd],
        out_shape=[out, out],
        scratch_shapes=[state, state],
        compiler_params=_params("parallel", "arbitrary"),
        name="hgrn2_scan",
    )(hq, kf, gf, hv, hq, kb, gb, hv)


def _hyb_out_body(x_ref, mod_ref, oa_ref, of_ref, ob_ref, hg_ref, nw_ref, w_ref, g_ref, b_ref, *ffn_and_out):
    o = of_ref[0] + ob_ref[0]
    heads = []
    for h in range(B_HEADS):
        oh = o[:, h * B_VAL_DIM:(h + 1) * B_VAL_DIM]
        heads.append(oh * lax.rsqrt(jnp.mean(oh * oh, -1, keepdims=True) + RMS_EPS))
    ohg = (jnp.concatenate(heads, axis=1) * nw_ref[...] * hg_ref[0]).astype(BF16)
    y = _dot(oa_ref[0], w_ref[0:A_Q_W, :]) + _dot(ohg, w_ref[A_Q_W:, :])
    x1 = _residual_ln(x_ref[0], y, mod_ref, g_ref, b_ref, 1.0)
    ffn_and_out[-1][0] = _ffn_rows(x1, *ffn_and_out[:-1])


def _hybrid_out_ffn(x, mod, oa, of, ob, hg, nw, w, g, b, ffn, tm):
    bsz, seq, d = x.shape
    tok = pl.BlockSpec((1, tm, d), lambda i, j: (i, j, 0))
    slab = pl.BlockSpec((1, tm, B_W), lambda i, j: (i, j, 0))
    return pl.pallas_call(
        _hyb_out_body,
        grid=(bsz, seq // tm),
        in_specs=[tok, pl.BlockSpec((1, 3, d), lambda i, j: (i, 0, 0)), slab, slab, slab, slab,
                  _resident(nw.shape), _resident(w.shape), _resident(g.shape), _resident(b.shape)]
                 + _ffn_specs(d, ffn),
        out_specs=tok,
        out_shape=jax.ShapeDtypeStruct(x.shape, F32),
        compiler_params=_params("parallel", "parallel"),
        name="hybrid_out_ffn",
    )(x, mod, oa, of, ob, hg, nw, w, g, b, *ffn)


def _rms(x, w_ref):
    return (x * lax.rsqrt(jnp.mean(x * x, -1, keepdims=True) + RMS_EPS) * w_ref[...]).astype(BF16)


def _mla_in_body(x_ref, mod_ref, pos_ref, freq_ref, sgn_ref, wd_ref, qn_ref, kvn_ref, wq_ref, wkv_ref,
                 q_ref, k_ref, v_ref):
    h = _modulate(x_ref[0], mod_ref).astype(BF16)
    down = _dot(h, wd_ref[...])
    ang = pos_ref[0].astype(F32) * freq_ref[...]
    cos = jnp.cos(ang)
    sin = jnp.sin(ang) * sgn_ref[...]
    off = C_Q_RANK + C_KV_RANK
    k_rope = down[:, off:off + LANES] * cos + down[:, off + LANES:off + 2 * LANES] * sin
    cq = _rms(down[:, :C_Q_RANK], qn_ref)
    ckv = _rms(down[:, C_Q_RANK:off], kvn_ref)
    qq = _dot(cq, wq_ref[...])
    kv = _dot(ckv, wkv_ref[...])
    scale = (C_NOPE + C_ROPE) ** -0.5 * LOG2_E
    nh = C_HEADS * LANES
    for hd in range(C_HEADS):
        lanes = slice(hd * LANES, (hd + 1) * LANES)
        q_ref[0, hd] = ((qq[:, lanes] * cos + qq[:, nh + hd * LANES:nh + (hd + 1) * LANES] * sin) * scale).astype(BF16)
        k_ref[0, hd] = (kv[:, lanes] + k_rope).astype(BF16)
    low = lax.broadcasted_iota(jnp.int32, (x_ref.shape[1], LANES), 1) < C_V
    for pr in range(C_HEADS // 2):
        pair = kv[:, nh + pr * LANES:nh + (pr + 1) * LANES]
        v_ref[0, 2 * pr] = jnp.where(low, pair, 1.0).astype(BF16)
        v_ref[0, 2 * pr + 1] = jnp.where(low, 1.0, pair).astype(BF16)


def _mla_in(x, mod, pos, freq, sgn, wd, qn, kvn, wq, wkv, tm):
    bsz, seq, d = x.shape
    tok = pl.BlockSpec((1, tm, d), lambda i, j: (i, j, 0))
    heads = lambda n: pl.BlockSpec((1, n, tm, LANES), lambda i, j: (i, 0, j, 0))
    sds = lambda n: jax.ShapeDtypeStruct((bsz, n, seq, LANES), BF16)
    return pl.pallas_call(
        _mla_in_body,
        grid=(bsz, seq // tm),
        in_specs=[tok, pl.BlockSpec((1, 3, d), lambda i, j: (i, 0, 0)),
                  pl.BlockSpec((1, tm, 1), lambda i, j: (i, j, 0)),
                  _resident(freq.shape), _resident(sgn.shape), _resident(wd.shape),
                  _resident(qn.shape), _resident(kvn.shape), _resident(wq.shape), _resident(wkv.shape)],
        out_specs=[heads(C_HEADS)] * 3,
        out_shape=[sds(C_HEADS)] * 3,
        compiler_params=_params("parallel", "parallel"),
        name="mla_in",
    )(x, mod, pos.reshape(bsz, seq, 1), freq, sgn, wd, qn, kvn, wq, wkv)


def _mla_scores(q, k, s_ref, m_ref):
    s = lax.dot_general(q, k, _NT, preferred_element_type=F32)
    m_ref[...] = jnp.max(s, -1, keepdims=True)
    s_ref[...] = s


def _mla_finish(s_ref, m_ref, v):
    p = jnp.exp2(s_ref[...] - m_ref[...]).astype(BF16)
    o = _dot(p, v)
    return o / pltpu.roll(o, C_V, 1)


def _mla_attn_body(q_ref, k_ref, vpo_ref, ve_ref, vo_ref, main_ref, last_ref,
                   s0_ref, s1_ref, m0_ref, m1_ref, o0_ref):
    @pl.when(pl.program_id(0) == 0)
    def _():
        s1_ref[...] = jnp.zeros_like(s1_ref)
        m1_ref[...] = jnp.zeros_like(m1_ref)
        o0_ref[...] = jnp.zeros_like(o0_ref)

    tq = s0_ref.shape[0]
    nt = q_ref.shape[2] // tq
    low = lax.broadcasted_iota(jnp.int32, (tq, LANES), 1) < C_V
    even = None
    for r in range(nt):
        rows = slice(r * tq, (r + 1) * tq)
        _mla_scores(q_ref[0, 0, rows, :], k_ref[0, 0], s0_ref, m0_ref)
        if r == 0:
            odd = _mla_finish(s1_ref, m1_ref, vpo_ref[0, 0])
            last_ref[0] = jnp.where(low, o0_ref[...], odd).astype(BF16)
        else:
            odd = _mla_finish(s1_ref, m1_ref, vo_ref[0, 0])
            main_ref[0, (r - 1) * tq:r * tq, :] = jnp.where(low, even, odd).astype(BF16)
        _mla_scores(q_ref[0, 1, rows, :], k_ref[0, 1], s1_ref, m1_ref)
        even = _mla_finish(s0_ref, m0_ref, ve_ref[0, 0])
    o0_ref[...] = even


def _mla_attention(q, k, v, tq, nt):
    bsz, nh, seq, _ = q.shape
    ni = seq // (nt * tq)
    n_items = bsz * (nh // 2) * ni

    def item(t):
        t = jnp.minimum(t, n_items - 1)
        return t // (ni * (nh // 2)), (t // ni) % (nh // 2), t % ni

    def prev(t):
        return item(jnp.maximum(t - 1, 0))

    head = lambda f, odd: pl.BlockSpec((1, 1, seq, LANES), lambda t: (f(t)[0], 2 * f(t)[1] + odd, 0, 0))
    tiles = lambda n, f: pl.BlockSpec((1, n * tq, LANES), lambda t: (f(t)[0], f(t)[2], f(t)[1]))
    out = lambda n: jax.ShapeDtypeStruct((bsz, ni * n * tq, nh * C_V), BF16)
    return pl.pallas_call(
        _mla_attn_body,
        grid=(n_items + 1,),
        in_specs=[pl.BlockSpec((1, 2, nt * tq, LANES), lambda t: (item(t)[0], item(t)[1], item(t)[2], 0)),
                  pl.BlockSpec((1, 2, seq, LANES), lambda t: (item(t)[0], item(t)[1], 0, 0)),
                  head(prev, 1), head(item, 0), head(item, 1)],
        out_specs=[tiles(nt - 1, item), tiles(1, prev)],
        out_shape=[out(nt - 1), out(1)],
        scratch_shapes=[pltpu.VMEM((tq, seq), F32), pltpu.VMEM((tq, seq), F32),
                        pltpu.VMEM((tq, 1), F32), pltpu.VMEM((tq, 1), F32), pltpu.VMEM((tq, LANES), F32)],
        compiler_params=_params("arbitrary"),
        name="mla_attention",
    )(q, k, v, v, v)


def _proj_out_body(nt, x_ref, mod_ref, main_ref, last_ref, w_ref, g_ref, b_ref, *ffn_and_out):
    att = jnp.where(pl.program_id(1) % nt == nt - 1, last_ref[0], main_ref[0])
    y = _dot(att, w_ref[...])
    x1 = _residual_ln(x_ref[0], y, mod_ref, g_ref, b_ref, 1.0)
    ffn_and_out[-1][0] = _ffn_rows(x1, *ffn_and_out[:-1])


def _proj_out_ffn(x, mod, att_main, att_last, nt, w, g, b, ffn, tm):
    bsz, seq, d = x.shape
    tok = pl.BlockSpec((1, tm, d), lambda i, j: (i, j, 0))
    width = att_main.shape[-1]
    main = pl.BlockSpec((1, tm, width), lambda i, j: (i, (j // nt) * (nt - 1) + jnp.minimum(j % nt, nt - 2), 0))
    last = pl.BlockSpec((1, tm, width), lambda i, j: (i, j // nt, 0))
    return pl.pallas_call(
        functools.partial(_proj_out_body, nt),
        grid=(bsz, seq // tm),
        in_specs=[tok, pl.BlockSpec((1, 3, d), lambda i, j: (i, 0, 0)), main, last,
                  _resident(w.shape), _resident(g.shape), _resident(b.shape)] + _ffn_specs(d, ffn),
        out_specs=tok,
        out_shape=jax.ShapeDtypeStruct(x.shape, F32),
        compiler_params=_params("parallel", "parallel"),
        name="mla_out_ffn",
    )(x, mod, att_main, att_last, w, g, b, *ffn)


def _hybrid_w_in(w):
    d = w.shape[0]
    aq = w[:, :A_Q_W]
    ak = w[:, A_Q_W:A_Q_W + A_KV_W].reshape(d, A_KV_HEADS, 1, A_HEAD_DIM)
    av = w[:, A_Q_W + A_KV_W:A_Q_W + 2 * A_KV_W].reshape(d, A_KV_HEADS, 1, A_HEAD_DIM)
    tile = lambda t: jnp.broadcast_to(t, (d, A_KV_HEADS, A_GROUP, A_HEAD_DIM)).reshape(d, A_Q_W)
    return jnp.concatenate([aq, tile(ak), tile(av), w[:, A_Q_W + 2 * A_KV_W:]], axis=1).astype(BF16)


def _swap_halves(t):
    half = t.shape[-1] // 2
    return jnp.concatenate([t[..., half:], t[..., :half]], axis=-1)


def _mla_weights(w_down, w_uq, w_ukv):
    d = w_down.shape[0]
    off = C_Q_RANK + C_KV_RANK
    pad_rope = lambda t: jnp.pad(t, [(0, 0)] * (t.ndim - 1) + [(C_NOPE, LANES - C_NOPE - C_ROPE)])
    kr = w_down[:, off:]
    wd = jnp.concatenate([w_down[:, :off], pad_rope(kr), pad_rope(_swap_halves(kr))], axis=1)
    uq = w_uq.reshape(C_Q_RANK, C_HEADS, C_NOPE + C_ROPE)
    q_main = jnp.pad(uq, [(0, 0), (0, 0), (0, LANES - C_NOPE - C_ROPE)])
    q_swap = pad_rope(_swap_halves(uq[..., C_NOPE:]))
    wq = jnp.concatenate([q_main.reshape(C_Q_RANK, -1), q_swap.reshape(C_Q_RANK, -1)], axis=1)
    ukv = w_ukv.reshape(C_KV_RANK, C_HEADS, C_NOPE + C_V)
    k_main = jnp.pad(ukv[..., :C_NOPE], [(0, 0), (0, 0), (0, LANES - C_NOPE)])
    wkv = jnp.concatenate([k_main.reshape(C_KV_RANK, -1), ukv[..., C_NOPE:].reshape(C_KV_RANK, -1)], axis=1)
    return wd.astype(BF16), wq.astype(BF16), wkv.astype(BF16)


def _rope_tables():
    half = C_ROPE // 2
    freqs = ROPE_THETA ** (-jnp.arange(half, dtype=F32) / half)
    zeros = jnp.zeros((C_NOPE,), F32)
    tail = jnp.zeros((LANES - C_NOPE - C_ROPE,), F32)
    freq = jnp.concatenate([zeros, freqs, freqs, tail]).reshape(1, LANES)
    ones = jnp.ones((half,), F32)
    sgn = jnp.concatenate([zeros, -ones, ones, tail]).reshape(1, LANES)
    return freq, sgn


def kernel(x, c, positions, ada_w, ada_b, ln_g, ln_b, ffn_w_gate, ffn_w_up, ffn_w_down,
           hyb_w_in, hyb_w_out, attn_sink, hgrn_lb_logits, hgrn_norm_w,
           mla_w_down, mla_q_norm, mla_kv_norm, mla_w_uq, mla_w_ukv, mla_w_out):
    bsz, seq, d = x.shape
    tm = min(512, seq)
    mod_all = _modulation(c, ada_w, ada_b).reshape(DEPTH, bsz, N_SUB, 3, d)
    row = lambda t: t.reshape(1, -1)
    freq, sgn = _rope_tables()
    for layer in range(DEPTH):
        mod = lambda j, l=layer: mod_all[l, :, j]
        ffn = lambda j, s, l=layer: (mod(s), ffn_w_gate[l, j].astype(BF16), ffn_w_up[l, j].astype(BF16),
                                     ffn_w_down[l, j].astype(BF16), row(ln_g[l, s]), row(ln_b[l, s]))
        x = _ffn_sublayer(x, ffn(0, 0), min(1024, seq))
        if layer % 2 == 0:
            e = layer // 2
            aq, ak, av, hq, kf, gf, kb, gb, hv, hg = _hybrid_in(
                x, mod(1), _hybrid_w_in(hyb_w_in[e]), hgrn_lb_logits.astype(F32), layer, tm)
            oa = _window_attention(aq, ak, av, positions, attn_sink[e].astype(F32))
            of, ob = _hgrn2(hq, kf, gf, kb, gb, hv, tm)
            x = _hybrid_out_ffn(x, mod(1), oa, of, ob, hg, row(hgrn_norm_w[e]), hyb_w_out[e].astype(BF16),
                                row(ln_g[layer, 1]), row(ln_b[layer, 1]), ffn(1, 2), tm)
        else:
            o = layer // 2
            wd, wq, wkv = _mla_weights(mla_w_down[o], mla_w_uq[o], mla_w_ukv[o])
            q, k, v = _mla_in(x, mod(1), positions, freq, sgn, wd, row(mla_q_norm[o]), row(mla_kv_norm[o]),
                              wq, wkv, tm)
            tq = min(512, seq // 2)
            nt = 4 if seq % (4 * tq) == 0 else 2
            att_main, att_last = _mla_attention(q, k, v, tq, nt)
            x = _proj_out_ffn(x, mod(1), att_main, att_last, nt, mla_w_out[o].astype(BF16),
                              row(ln_g[layer, 1]), row(ln_b[layer, 1]), ffn(1, 2), tq)
    return x
```
